```python
import jax
import jax.numpy as jnp
from jax import lax
import numpy as np

D_MODEL = 1024
BATCH = 2
SEQ = 8192
DEPTH = 4

GRID_W = 64
CTX_LEN = 256
HEAD_DIM = 64
ATT_HEADS = 8
KV_HEADS = 2
Q_PER_KV = ATT_HEADS // KV_HEADS
ATT_W = ATT_HEADS * HEAD_DIM
KV_W = KV_HEADS * HEAD_DIM
WINDOW = 128
ROPE_THETA = 10000.0
CHUNK = 128
SGU_GROUPS = 4
SGU_GW = 128
SGU_W = SGU_GROUPS * SGU_GW
LRU_W = D_MODEL
LRU_BLOCKS = 8
LRU_BW = LRU_W // LRU_BLOCKS
CONV_W = 4
LRU_C = 8.0
N_BRANCHES = 3
N_GROUPS = 4
EXPERTS_PER_GROUP = 8
N_EXPERTS = N_GROUPS * EXPERTS_PER_GROUP
TOP_K = 2
D_EXPERT = 512
MOE_BLOCK = 128
N_MOD = 6
COL_WIDTHS = (ATT_W, SGU_W, SGU_W, N_BRANCHES * D_MODEL, LRU_W, KV_W, KV_W, LRU_W)
IN_W = ATT_W + 2 * SGU_W + N_BRANCHES * D_MODEL + LRU_W + 2 * KV_W + LRU_W
CTX_COL_OFF = IN_W - (2 * KV_W + LRU_W)
ALPHA = (2.0 * DEPTH) ** 0.25
BETA = (8.0 * DEPTH) ** -0.25
LN_EPS = 1e-6
NEG_INF = -1e30

kernel_name = 'hybrid_gated_branch_dit_trunk'


def layer_norm(x, g, b):
    xf = x.astype(jnp.float32)
    mu = jnp.mean(xf, -1, keepdims=True)
    var = jnp.mean(jnp.square(xf - mu), -1, keepdims=True)
    y = (xf - mu) * lax.rsqrt(var + LN_EPS)
    return (y * g.astype(jnp.float32) + b.astype(jnp.float32)).astype(x.dtype)


def split_cols(t, widths):
    return jnp.split(t, np.cumsum(widths)[:-1].tolist(), axis=-1)


def axial_rope_tables(n_tok):
    rows = n_tok // GRID_W
    row = jnp.repeat(jnp.arange(rows, dtype=jnp.float32), GRID_W)
    col = jnp.tile(jnp.arange(GRID_W, dtype=jnp.float32), rows)
    axis_dim = HEAD_DIM // 2
    inv_freq = ROPE_THETA ** (-jnp.arange(0, axis_dim, 2, dtype=jnp.float32) / axis_dim)
    ang_r = row[:, None] * inv_freq
    ang_c = col[:, None] * inv_freq
    return (jnp.cos(ang_r), jnp.sin(ang_r), jnp.cos(ang_c), jnp.sin(ang_c))


def _rotate(x, cos, sin):
    f = cos.shape[-1]
    x1, x2 = x[..., :f], x[..., f:]
    cos = cos[:, None, :]
    sin = sin[:, None, :]
    return jnp.concatenate([x1 * cos - x2 * sin, x2 * cos + x1 * sin], -1)


def apply_axial_rope(x, tables):
    cr, sr, cc, sc = tables
    xf = x.astype(jnp.float32)
    half = HEAD_DIM // 2
    out = jnp.concatenate([_rotate(xf[..., :half], cr, sr), _rotate(xf[..., half:], cc, sc)], -1)
    return out.astype(x.dtype)


def spatial_gating(u, v, ln_g, ln_b, w_spatial, b_spatial):
    bsz, n = u.shape[:2]
    n_chunks = n // CHUNK
    vn = layer_norm(v, ln_g, ln_b).reshape(bsz, n_chunks, CHUNK, SGU_GROUPS, SGU_GW)
    mixed = jnp.einsum('gpq,bnqgc->bnpgc', w_spatial, vn) + b_spatial.T[:, :, None]
    return u * mixed.reshape(bsz, n, SGU_W)


def window_attention(q, k, v, k_ctx, v_ctx, sink):
    bsz, n = q.shape[:2]
    nb = n // WINDOW
    scale = HEAD_DIM ** -0.5
    qb = q.reshape(bsz, nb, WINDOW, KV_HEADS, Q_PER_KV, HEAD_DIM)

    def neighbours(t):
        tp = jnp.pad(t, ((0, 0), (WINDOW, WINDOW), (0, 0), (0, 0)))
        tp = tp.reshape(bsz, nb + 2, WINDOW, KV_HEADS, HEAD_DIM)
        return jnp.concatenate([tp[:, :-2], tp[:, 1:-1], tp[:, 2:]], axis=2)

    kb, vb = neighbours(k), neighbours(v)
    s_loc = jnp.einsum('bnqkgd,bnjkd->bnkgqj', qb, kb, preferred_element_type=jnp.float32) * scale
    s_ctx = jnp.einsum('bnqkgd,bckd->bnkgqc', qb, k_ctx, preferred_element_type=jnp.float32) * scale
    q_idx = jnp.arange(nb)[:, None, None] * WINDOW + jnp.arange(WINDOW)[None, :, None]
    k_idx = (jnp.arange(nb)[:, None, None] - 1) * WINDOW + jnp.arange(3 * WINDOW)[None, None, :]
    valid = (jnp.abs(k_idx - q_idx) <= WINDOW) & (k_idx >= 0) & (k_idx < n)
    s_loc = jnp.where(valid[None, :, None, None], s_loc, NEG_INF)
    sink_l = jnp.broadcast_to(sink.astype(jnp.float32).reshape(KV_HEADS, Q_PER_KV)[:, :, None, None],
                              s_loc.shape[:-1] + (1,))
    n_ctx = k_ctx.shape[1]
    p = jax.nn.softmax(jnp.concatenate([sink_l, s_ctx, s_loc], -1), axis=-1)
    p_ctx = p[..., 1:1 + n_ctx].astype(v.dtype)
    p_loc = p[..., 1 + n_ctx:].astype(v.dtype)
    o = (jnp.einsum('bnkgqc,bckd->bnqkgd', p_ctx, v_ctx)
         + jnp.einsum('bnkgqj,bnjkd->bnqkgd', p_loc, vb))
    return o.reshape(bsz, n, ATT_W)


def context_attention(q, k, v, sink):
    bsz, n = q.shape[:2]
    qg = q.reshape(bsz, n, KV_HEADS, Q_PER_KV, HEAD_DIM)
    s = jnp.einsum('bqkgd,bckd->bkgqc', qg, k, preferred_element_type=jnp.float32) * HEAD_DIM ** -0.5
    sink_l = jnp.broadcast_to(sink.astype(jnp.float32).reshape(KV_HEADS, Q_PER_KV)[:, :, None, None],
                              s.shape[:-1] + (1,))
    p = jax.nn.softmax(jnp.concatenate([sink_l, s], -1), axis=-1)[..., 1:].astype(v.dtype)
    o = jnp.einsum('bkgqc,bckd->bqkgd', p, v)
    return o.reshape(bsz, n, ATT_W)


def short_conv(t, w, b, reverse):
    n = t.shape[1]
    pad = (0, CONV_W - 1) if reverse else (CONV_W - 1, 0)
    tp = jnp.pad(t, ((0, 0), pad, (0, 0)))
    out = b
    for j in range(CONV_W):
        out = out + w[j] * tp[:, j:j + n]
    return out


def block_diag(t, w, b):
    tb = t.reshape(t.shape[:-1] + (LRU_BLOCKS, LRU_BW))
    return jnp.einsum('blnc,ncd->blnd', tb, w).reshape(t.shape) + b


def lru_gates(t, w_r, b_r, w_i, b_i, lam):
    r = jax.nn.sigmoid(block_diag(t, w_r, b_r).astype(jnp.float32))
    i = jax.nn.sigmoid(block_diag(t, w_i, b_i).astype(jnp.float32))
    log_a = -LRU_C * r * jax.nn.softplus(-lam.astype(jnp.float32))
    a = jnp.exp(log_a)
    mult = jnp.sqrt(-jnp.expm1(2.0 * log_a))
    return a, mult * i * t.astype(jnp.float32)


def linear_scan(a, u, h0):
    def combine(left, right):
        return (left[0] * right[0], right[0] * left[1] + right[1])
    a_cum, h = lax.associative_scan(combine, (a, u), axis=1)
    return h + a_cum * h0[:, None, :]


def rglru_direction(x_ctx, x_lat, conv_w, conv_b, w_r, b_r, w_i, b_i, lam, reverse):
    def scan_inputs(t):
        a, u = lru_gates(short_conv(t, conv_w, conv_b, reverse), w_r, b_r, w_i, b_i, lam)
        if reverse:
            return jnp.flip(a, 1), jnp.flip(u, 1)
        return a, u

    a_c, u_c = scan_inputs(x_ctx)
    a_l, u_l = scan_inputs(x_lat)
    h_c = linear_scan(a_c, u_c, jnp.zeros((x_ctx.shape[0], LRU_W), jnp.float32))
    h_l = linear_scan(a_l, u_l, h_c[:, -1])
    if reverse:
        return jnp.flip(h_c, 1), jnp.flip(h_l, 1)
    return h_c, h_l


def merge_branches(a, b, r, gate_logits, w_proj_a, w_proj_b, w_proj_c, w_out, b_out):
    g_a, g_b, g_r = jnp.split(jax.nn.sigmoid(gate_logits), N_BRANCHES, axis=-1)
    y = g_a * (a @ w_proj_a) + g_b * (b @ w_proj_b) + g_r * (r @ w_proj_c)
    return y @ w_out + b_out


def token_mixer(h_lat, h_ctx, rope, w_in, b_in, sgu_ln_g, sgu_ln_b, w_spatial, b_spatial, attn_sink,
                conv_w, conv_b, w_rgate, b_rgate, w_igate, b_igate, lru_lambda,
                w_proj_a, w_proj_b, w_proj_c, w_out, b_out, need_ctx_out):
    q, u, v, gates, z, k, val, xr = split_cols(h_lat @ w_in + b_in, COL_WIDTHS)
    if need_ctx_out:
        q_c, u_c, v_c, gates_c, z_c, k_c, val_c, xr_c = split_cols(h_ctx @ w_in + b_in, COL_WIDTHS)
    else:
        k_c, val_c, xr_c = split_cols(h_ctx @ w_in[:, CTX_COL_OFF:] + b_in[CTX_COL_OFF:], COL_WIDTHS[-3:])

    def heads(t, nh):
        return t.reshape(t.shape[:2] + (nh, HEAD_DIM))

    a_lat = spatial_gating(jax.nn.gelu(u), jax.nn.gelu(v), sgu_ln_g, sgu_ln_b, w_spatial, b_spatial)
    k_cx, v_cx = heads(k_c, KV_HEADS), heads(val_c, KV_HEADS)
    b_lat = window_attention(apply_axial_rope(heads(q, ATT_HEADS), rope),
                             apply_axial_rope(heads(k, KV_HEADS), rope),
                             heads(val, KV_HEADS), k_cx, v_cx, attn_sink)
    hc_f, hl_f = rglru_direction(xr_c, xr, conv_w[0], conv_b[0], w_rgate[0], b_rgate[0],
                                 w_igate[0], b_igate[0], lru_lambda[0], reverse=False)
    hc_b, hl_b = rglru_direction(xr_c, xr, conv_w[1], conv_b[1], w_rgate[1], b_rgate[1],
                                 w_igate[1], b_igate[1], lru_lambda[1], reverse=True)
    r_lat = (hl_f + hl_b).astype(z.dtype) * jax.nn.gelu(z)
    y_lat = merge_branches(a_lat, b_lat, r_lat, gates, w_proj_a, w_proj_b, w_proj_c, w_out, b_out)
    if not need_ctx_out:
        return y_lat, None
    a_ctx = spatial_gating(jax.nn.gelu(u_c), jax.nn.gelu(v_c), sgu_ln_g, sgu_ln_b, w_spatial, b_spatial)
    b_ctx = context_attention(heads(q_c, ATT_HEADS), k_cx, v_cx, attn_sink)
    r_ctx = (hc_f + hc_b).astype(z_c.dtype) * jax.nn.gelu(z_c)
    y_ctx = merge_branches(a_ctx, b_ctx, r_ctx, gates_c, w_proj_a, w_proj_b, w_proj_c, w_out, b_out)
    return y_lat, y_ctx


def hierarchical_moe(h, w_group, b_group, w_router, b_router, w1, w3, w2):
    bsz, n, d = h.shape
    xt = h.reshape(-1, d)
    n_tok = xt.shape[0]
    g_logits = (xt @ w_group + b_group).astype(jnp.float32)
    g_idx = jnp.argmax(g_logits, axis=-1)
    g_w = jnp.take_along_axis(jax.nn.softmax(g_logits, axis=-1), g_idx[:, None], axis=-1)[:, 0]
    e_logits = (xt @ w_router + b_router).astype(jnp.float32).reshape(n_tok, N_GROUPS, EXPERTS_PER_GROUP)
    e_logits = jnp.take_along_axis(e_logits, g_idx[:, None, None], axis=1)[:, 0]
    top_v, top_j = lax.top_k(e_logits, TOP_K)
    gate = g_w[:, None] * jax.nn.softmax(top_v, axis=-1)
    expert = g_idx[:, None] * EXPERTS_PER_GROUP + top_j
    n_as = n_tok * TOP_K
    e_flat = expert.reshape(-1)
    order = jnp.argsort(e_flat)
    e_sorted = e_flat[order]
    tok_sorted = order // TOP_K
    gate_sorted = gate.reshape(-1)[order]
    counts = jnp.bincount(e_flat, length=N_EXPERTS)
    starts = jnp.cumsum(counts) - counts
    padded = (counts + MOE_BLOCK - 1) // MOE_BLOCK * MOE_BLOCK
    pad_ends = jnp.cumsum(padded)
    pad_starts = pad_ends - padded
    slot = pad_starts[e_sorted] + jnp.arange(n_as) - starts[e_sorted]
    n_slots = (-(-n_as // MOE_BLOCK) + N_EXPERTS) * MOE_BLOCK
    n_blocks = n_slots // MOE_BLOCK
    buf = jnp.zeros((n_slots, d), xt.dtype).at[slot].set(xt[tok_sorted])
    block_expert = jnp.minimum(
        jnp.searchsorted(pad_ends, jnp.arange(n_blocks) * MOE_BLOCK, side='right'), N_EXPERTS - 1)

    def expert_block(args):
        xb, e = args
        return (jax.nn.silu(xb @ w1[e]) * (xb @ w3[e])) @ w2[e]

    out = lax.map(expert_block, (buf.reshape(n_blocks, MOE_BLOCK, d), block_expert)).reshape(n_slots, d)
    contrib = out[slot] * gate_sorted[:, None].astype(out.dtype)
    y = jax.ops.segment_sum(contrib, tok_sorted, num_segments=n_tok)
    return y.reshape(bsz, n, d)


def setup_inputs(seed: int = 0) -> dict:
    key = jax.random.key(seed)
    keys = iter(jax.random.split(key, 48))
    f32 = jnp.float32
    L = DEPTH
    D = D_MODEL

    def nrm(shape, scale):
        return jax.random.normal(next(keys), shape, f32) * scale

    x = nrm((BATCH, SEQ, D), 1.0)
    c = nrm((BATCH, D), 1.0)
    ctx = nrm((BATCH, CTX_LEN, D), 1.0)
    c_ctx = nrm((D,), 1.0)
    w_mod = nrm((L, D, N_MOD * D), 0.5 * D ** -0.5)
    b_mod = nrm((L, N_MOD * D), 0.02)
    w_in = nrm((L, D, IN_W), D ** -0.5)
    b_in = nrm((L, IN_W), 0.02)
    sgu_ln_g = 1.0 + nrm((L, SGU_W), 0.02)
    sgu_ln_b = nrm((L, SGU_W), 0.02)
    w_spatial = nrm((L, SGU_GROUPS, CHUNK, CHUNK), CHUNK ** -0.5)
    b_spatial = 1.0 + nrm((L, SGU_GROUPS, CHUNK), 0.02)
    attn_sink = nrm((L, ATT_HEADS), 0.5)
    conv_w = nrm((L, 2, CONV_W, LRU_W), CONV_W ** -0.5)
    conv_b = nrm((L, 2, LRU_W), 0.02)
    w_rgate = nrm((L, 2, LRU_BLOCKS, LRU_BW, LRU_BW), LRU_BW ** -0.5)
    b_rgate = nrm((L, 2, LRU_W), 0.02)
    w_igate = nrm((L, 2, LRU_BLOCKS, LRU_BW, LRU_BW), LRU_BW ** -0.5)
    b_igate = nrm((L, 2, LRU_W), 0.02)
    a_pow = jax.random.uniform(next(keys), (L, 2, LRU_W), f32, 0.9, 0.999)
    a_base = a_pow ** (1.0 / LRU_C)
    lru_lambda = jnp.log(a_base) - jnp.log1p(-a_base)
    w_proj_a = nrm((L, SGU_W, D), SGU_W ** -0.5)
    w_proj_b = nrm((L, ATT_W, D), ATT_W ** -0.5)
    w_proj_c = nrm((L, LRU_W, D), LRU_W ** -0.5)
    w_out = nrm((L, D, D), BETA * D ** -0.5)
    b_out = nrm((L, D), 0.02)
    ln1_g = 1.0 + nrm((L, D), 0.02)
    ln1_b = nrm((L, D), 0.02)
    ln2_g = 1.0 + nrm((L, D), 0.02)
    ln2_b = nrm((L, D), 0.02)
    w_group = nrm((L, D, N_GROUPS), D ** -0.5)
    b_group = nrm((L, N_GROUPS), 0.01)
    w_router = nrm((L, D, N_EXPERTS), D ** -0.5)
    b_router = nrm((L, N_EXPERTS), 0.01)
    w1 = nrm((L, N_EXPERTS, D, D_EXPERT), D ** -0.5)
    w3 = nrm((L, N_EXPERTS, D, D_EXPERT), D ** -0.5)
    w2 = nrm((L, N_EXPERTS, D_EXPERT, D), BETA * D_EXPERT ** -0.5)
    return {'x': x, 'c': c, 'ctx': ctx, 'c_ctx': c_ctx, 'w_mod': w_mod, 'b_mod': b_mod,
            'w_in': w_in, 'b_in': b_in, 'sgu_ln_g': sgu_ln_g, 'sgu_ln_b': sgu_ln_b,
            'w_spatial': w_spatial, 'b_spatial': b_spatial, 'attn_sink': attn_sink,
            'conv_w': conv_w, 'conv_b': conv_b, 'w_rgate': w_rgate, 'b_rgate': b_rgate,
            'w_igate': w_igate, 'b_igate': b_igate, 'lru_lambda': lru_lambda,
            'w_proj_a': w_proj_a, 'w_proj_b': w_proj_b, 'w_proj_c': w_proj_c,
            'w_out': w_out, 'b_out': b_out, 'ln1_g': ln1_g, 'ln1_b': ln1_b,
            'ln2_g': ln2_g, 'ln2_b': ln2_b, 'w_group': w_group, 'b_group': b_group,
            'w_router': w_router, 'b_router': b_router, 'w1': w1, 'w3': w3, 'w2': w2}


def reference(x, c, ctx, c_ctx, w_mod, b_mod, w_in, b_in, sgu_ln_g, sgu_ln_b, w_spatial, b_spatial,
              attn_sink, conv_w, conv_b, w_rgate, b_rgate, w_igate, b_igate, lru_lambda,
              w_proj_a, w_proj_b, w_proj_c, w_out, b_out, ln1_g, ln1_b, ln2_g, ln2_b,
              w_group, b_group, w_router, b_router, w1, w3, w2):
    n_ctx = ctx.shape[1]
    rope = axial_rope_tables(x.shape[1])
    cond_lat = jax.nn.silu(c)
    cond_ctx = jax.nn.silu(c_ctx)
    for l in range(DEPTH):
        last = l == DEPTH - 1
        m_lat = [m[:, None, :] for m in jnp.split(cond_lat @ w_mod[l] + b_mod[l], N_MOD, axis=-1)]
        m_ctx = jnp.split(cond_ctx @ w_mod[l] + b_mod[l], N_MOD, axis=-1)
        y_lat, y_ctx = token_mixer(
            x * (1.0 + m_lat[1]) + m_lat[0], ctx * (1.0 + m_ctx[1]) + m_ctx[0], rope,
            w_in[l], b_in[l], sgu_ln_g[l], sgu_ln_b[l], w_spatial[l], b_spatial[l], attn_sink[l],
            conv_w[l], conv_b[l], w_rgate[l], b_rgate[l], w_igate[l], b_igate[l], lru_lambda[l],
            w_proj_a[l], w_proj_b[l], w_proj_c[l], w_out[l], b_out[l], need_ctx_out=not last)
        x = layer_norm(ALPHA * x + m_lat[2] * y_lat, ln1_g[l], ln1_b[l])
        h_lat = x * (1.0 + m_lat[4]) + m_lat[3]
        if last:
            f_lat = hierarchical_moe(h_lat, w_group[l], b_group[l], w_router[l], b_router[l],
                                     w1[l], w3[l], w2[l])
        else:
            ctx = layer_norm(ALPHA * ctx + m_ctx[2] * y_ctx, ln1_g[l], ln1_b[l])
            h_ctx = ctx * (1.0 + m_ctx[4]) + m_ctx[3]
            f_all = hierarchical_moe(jnp.concatenate([h_ctx, h_lat], axis=1), w_group[l], b_group[l],
                                     w_router[l], b_router[l], w1[l], w3[l], w2[l])
            f_lat = f_all[:, n_ctx:]
            ctx = layer_norm(ALPHA * ctx + m_ctx[5] * f_all[:, :n_ctx], ln2_g[l], ln2_b[l])
        x = layer_norm(ALPHA * x + m_lat[5] * f_lat, ln2_g[l], ln2_b[l])
    return x
```

```python
import functools

import jax
import jax.numpy as jnp
from jax import lax
from jax.experimental import pallas as pl
from jax.experimental.pallas import tpu as pltpu

F32 = jnp.float32
BF16 = jnp.bfloat16

D_MODEL = 1024
DEPTH = 4
GRID_W = 64
CTX_LEN = 256
HEAD_DIM = 64
ATT_HEADS = 8
KV_HEADS = 2
Q_PER_KV = ATT_HEADS // KV_HEADS
ATT_W = ATT_HEADS * HEAD_DIM
KV_W = KV_HEADS * HEAD_DIM
WINDOW = 128
ROPE_THETA = 10000.0
CHUNK = 128
SGU_GROUPS = 4
SGU_W = SGU_GROUPS * CHUNK
LRU_W = D_MODEL
LRU_BLOCKS = 8
LRU_BW = LRU_W // LRU_BLOCKS
CONV_W = 4
LRU_C = 8.0
N_GROUPS = 4
EXPERTS_PER_GROUP = 8
N_EXPERTS = N_GROUPS * EXPERTS_PER_GROUP
TOP_K = 2
D_EXPERT = 512
N_MOD = 6
ALPHA = (2.0 * DEPTH) ** 0.25
LN_EPS = 1e-6
NEG_INF = -1e30

Q0, U0, V0, G0, Z0 = 0, ATT_W, ATT_W + SGU_W, ATT_W + 2 * SGU_W, ATT_W + 2 * SGU_W + 3 * D_MODEL
MAIN_W = Z0 + LRU_W
CTX_W = 2 * KV_W + LRU_W

TILE = 256
SUBLANES = 8
LANES = 128
EXPERT_BLOCK = 256
MIB = 1024 * 1024
VMEM_LIMIT = 56 * MIB


def _sigmoid(x):
    return 0.5 * (1.0 + jnp.tanh(0.5 * x))


def _gelu(x):
    return 0.5 * x * (1.0 + jnp.tanh(0.7978845608028654 * (x + 0.044715 * (x * x * x))))


def _layer_norm(x, g, b):
    mu = jnp.mean(x, axis=-1, keepdims=True)
    xc = x - mu
    var = jnp.mean(xc * xc, axis=-1, keepdims=True)
    return xc * lax.rsqrt(var + LN_EPS) * g + b


def _dot(a, b):
    return jnp.dot(a, b, preferred_element_type=F32)


def _dot3(a, b):
    a_hi = a.astype(BF16)
    a_lo = (a - a_hi.astype(F32)).astype(BF16)
    b_hi = b.astype(BF16)
    b_lo = (b - b_hi.astype(F32)).astype(BF16)
    return _dot(a_hi, b_hi) + (_dot(a_hi, b_lo) + _dot(a_lo, b_hi))


def _rope(x, cos, sin_signed):
    n = x.shape[1] // LANES
    if n > 1:
        cos = jnp.concatenate([cos] * n, axis=1)
        sin_signed = jnp.concatenate([sin_signed] * n, axis=1)
    lane = lax.broadcasted_iota(jnp.int32, x.shape, 1)
    first = (lane & 31) < 16
    w = x.shape[1]
    partner = jnp.where(first, pltpu.roll(x, w - 16, 1), pltpu.roll(x, 16, 1))
    return x * cos + partner * sin_signed


def _mods_kernel(c_ref, w_ref, b_ref, o_ref):
    c = c_ref[...]
    o_ref[...] = _dot3(c * _sigmoid(c), w_ref[...]) + b_ref[...]


def _modulation(cond, w_mod, b_mod):
    n_layers = w_mod.shape[0]
    tn = 1536
    return pl.pallas_call(
        _mods_kernel,
        out_shape=jax.ShapeDtypeStruct((n_layers, SUBLANES, N_MOD * D_MODEL), F32),
        grid=(n_layers, N_MOD * D_MODEL // tn),
        in_specs=[
            pl.BlockSpec((SUBLANES, D_MODEL), lambda l, j: (0, 0)),
            pl.BlockSpec((None, D_MODEL, tn), lambda l, j: (l, 0, j)),
            pl.BlockSpec((None, 1, tn), lambda l, j: (l, 0, j)),
        ],
        out_specs=pl.BlockSpec((None, SUBLANES, tn), lambda l, j: (l, 0, j)),
        compiler_params=pltpu.CompilerParams(dimension_semantics=("arbitrary", "arbitrary"),
                                             vmem_limit_bytes=40 * MIB),
        name="modulation",
    )(cond, w_mod, b_mod.reshape(n_layers, 1, N_MOD * D_MODEL))


def _scan_tile(a_ref, u_ref, carry_ref, out_ref, reverse):
    width = a_ref.shape[1]
    row = lax.broadcasted_iota(jnp.int32, (SUBLANES, width), 0)
    n_blk = a_ref.shape[0] // SUBLANES

    def body(j, carry):
        blk = (n_blk - 1 - j) if reverse else j
        r0 = pl.multiple_of(blk * SUBLANES, SUBLANES)
        a = a_ref[pl.ds(r0, SUBLANES), :]
        u = u_ref[pl.ds(r0, SUBLANES), :]
        for s in (1, 2, 4):
            if reverse:
                keep = row < SUBLANES - s
                a_sh = jnp.where(keep, pltpu.roll(a, SUBLANES - s, 0), 1.0)
                u_sh = jnp.where(keep, pltpu.roll(u, SUBLANES - s, 0), 0.0)
            else:
                keep = row >= s
                a_sh = jnp.where(keep, pltpu.roll(a, s, 0), 1.0)
                u_sh = jnp.where(keep, pltpu.roll(u, s, 0), 0.0)
            u = a * u_sh + u
            a = a * a_sh
        h = u + a * carry
        out_ref[pl.ds(r0, SUBLANES), :] = h
        return h[0:1, :] if reverse else h[SUBLANES - 1:SUBLANES, :]

    carry_ref[...] = lax.fori_loop(0, n_blk, body, carry_ref[...])


def _lru_inputs(t, wri_ref, br_ref, bi_ref, sp, a_ref, u_ref):
    for n in range(LRU_BLOCKS):
        cols = slice(n * LRU_BW, (n + 1) * LRU_BW)
        tn = t[:, cols]
        g = _dot(tn.astype(BF16), wri_ref[n])
        r = _sigmoid(g[:, :LRU_BW] + br_ref[:, cols])
        i = _sigmoid(g[:, LRU_BW:] + bi_ref[:, cols])
        a = jnp.exp((-LRU_C) * r * sp[:, cols])
        a_ref[:, cols] = a
        u_ref[:, cols] = jnp.sqrt(1.0 - a * a) * i * tn


def _kv_lru_kernel(ly_ref, xf_ref, xb_ref, mod_ref, wc_ref, bc_ref, cos_ref, sin_ref,
                   cw_ref, cb_ref, wri_ref, br_ref, bi_ref, lam_ref,
                   k_ref, v_ref, hf_ref, hb_ref,
                   xsf, xsb, af, uf, ab, ub, cf, cbk):
    del ly_ref
    i = pl.program_id(1)
    m = mod_ref[...]
    shift, scale = m[0:1, :], 1.0 + m[1:2, :]

    @pl.when(i == 0)
    def _():
        cf[...] = jnp.zeros_like(cf)
        cbk[...] = jnp.zeros_like(cbk)

    h = (xf_ref[...] * scale + shift).astype(BF16)
    p = _dot(h, wc_ref[...]) + bc_ref[...]
    k_ref[...] = _rope(p[:, :KV_W], cos_ref[...], sin_ref[...]).astype(BF16)
    v_ref[...] = p[:, KV_W:2 * KV_W].astype(BF16)
    xr_f = p[:, 2 * KV_W:]
    h = (xb_ref[...] * scale + shift).astype(BF16)
    xr_b = _dot(h, wc_ref[:, 2 * KV_W:]) + bc_ref[:, 2 * KV_W:]

    @pl.when(i <= 1)
    def _():
        xsf[0:SUBLANES, :] = jnp.zeros((SUBLANES, LRU_W), F32)
        xsb[TILE:TILE + SUBLANES, :] = jnp.zeros((SUBLANES, LRU_W), F32)

    @pl.when(i > 1)
    def _():
        xsf[0:SUBLANES, :] = xsf[TILE:TILE + SUBLANES, :]
        xsb[TILE:TILE + SUBLANES, :] = xsb[0:SUBLANES, :]

    xsf[SUBLANES:SUBLANES + TILE, :] = xr_f
    xsb[0:TILE, :] = xr_b

    lam = lam_ref[...]
    sp = jnp.maximum(-lam, 0.0) + jnp.log1p(jnp.exp(-jnp.abs(lam)))

    cw = cw_ref[0]
    t = (cb_ref[0] + cw[0:1, :] * xsf[SUBLANES - 3:SUBLANES - 3 + TILE, :]
         + cw[1:2, :] * xsf[SUBLANES - 2:SUBLANES - 2 + TILE, :]
         + cw[2:3, :] * xsf[SUBLANES - 1:SUBLANES - 1 + TILE, :]
         + cw[3:4, :] * xr_f)
    _lru_inputs(t, wri_ref.at[0], br_ref.at[0], bi_ref.at[0], sp[0:1, :], af, uf)
    cw = cw_ref[1]
    t = (cb_ref[1] + cw[0:1, :] * xr_b
         + cw[1:2, :] * xsb[1:1 + TILE, :]
         + cw[2:3, :] * xsb[2:2 + TILE, :]
         + cw[3:4, :] * xsb[3:3 + TILE, :])
    _lru_inputs(t, wri_ref.at[1], br_ref.at[1], bi_ref.at[1], sp[1:2, :], ab, ub)

    _scan_tile(af, uf, cf, hf_ref, reverse=False)
    _scan_tile(ab, ub, cbk, hb_ref, reverse=True)


def _kv_lru(lyr, xc, mods, w_ctx, b_ctx, cos_t, sin_t, conv_w, conv_b, w_ri, b_r, b_i, lam):
    bsz, n_tok, _ = xc.shape
    nt = n_tok // TILE
    tile_f = lambda b, i, ly: (b, i, 0)
    tile_b = lambda b, i, ly: (b, jnp.where(i == 0, 0, nt - i), 0)
    lay3 = lambda b, i, ly: (ly[0], 0, 0)
    lay4 = lambda b, i, ly: (ly[0], 0, 0, 0)
    return pl.pallas_call(
        _kv_lru_kernel,
        out_shape=(jax.ShapeDtypeStruct((bsz, n_tok, KV_W), BF16),
                   jax.ShapeDtypeStruct((bsz, n_tok, KV_W), BF16),
                   jax.ShapeDtypeStruct((bsz, n_tok, LRU_W), F32),
                   jax.ShapeDtypeStruct((bsz, n_tok, LRU_W), F32)),
        grid_spec=pltpu.PrefetchScalarGridSpec(
            num_scalar_prefetch=1,
            grid=(bsz, nt),
            in_specs=[
                pl.BlockSpec((None, TILE, D_MODEL), tile_f),
                pl.BlockSpec((None, TILE, D_MODEL), tile_b),
                pl.BlockSpec((None, None, N_MOD, D_MODEL),
                             lambda b, i, ly: (ly[0], jnp.where(i == 0, bsz, b), 0, 0)),
                pl.BlockSpec((None, D_MODEL, CTX_W), lay3),
                pl.BlockSpec((None, 1, CTX_W), lay3),
                pl.BlockSpec((TILE, KV_W), lambda b, i, ly: (i, 0)),
                pl.BlockSpec((TILE, KV_W), lambda b, i, ly: (i, 0)),
                pl.BlockSpec((None, 2, CONV_W, LRU_W), lay4),
                pl.BlockSpec((None, 2, 1, LRU_W), lay4),
                pl.BlockSpec((None, 2, LRU_BLOCKS, LRU_BW, 2 * LRU_BW), lambda b, i, ly: (ly[0], 0, 0, 0, 0)),
                pl.BlockSpec((None, 2, 1, LRU_W), lay4),
                pl.BlockSpec((None, 2, 1, LRU_W), lay4),
                pl.BlockSpec((None, 2, LRU_W), lay3),
            ],
            out_specs=(pl.BlockSpec((None, TILE, KV_W), tile_f),
                       pl.BlockSpec((None, TILE, KV_W), tile_f),
                       pl.BlockSpec((None, TILE, LRU_W), tile_f),
                       pl.BlockSpec((None, TILE, LRU_W), tile_b)),
            scratch_shapes=[pltpu.VMEM((TILE + SUBLANES, LRU_W), F32),
                            pltpu.VMEM((TILE + SUBLANES, LRU_W), F32),
                            pltpu.VMEM((TILE, LRU_W), F32), pltpu.VMEM((TILE, LRU_W), F32),
                            pltpu.VMEM((TILE, LRU_W), F32), pltpu.VMEM((TILE, LRU_W), F32),
                            pltpu.VMEM((1, LRU_W), F32), pltpu.VMEM((1, LRU_W), F32)],
        ),
        compiler_params=pltpu.CompilerParams(dimension_semantics=("arbitrary", "arbitrary"),
                                             vmem_limit_bytes=VMEM_LIMIT),
        name="kv_lru",
    )(lyr, xc, xc, mods, w_ctx, b_ctx, cos_t, sin_t, conv_w, conv_b, w_ri, b_r, b_i, lam)


def _attn_group(q, g, k_all, v_all, mask, sink):
    rows = q.shape[0]
    lane = lax.broadcasted_iota(jnp.int32, (rows, LANES), 1)
    mine = (lane >= g * HEAD_DIM) & (lane < (g + 1) * HEAD_DIM)
    parts = []
    for j in range(Q_PER_KV):
        hd = g * Q_PER_KV + j
        t = q[:, (hd // 2) * LANES:(hd // 2 + 1) * LANES]
        if hd % 2 != g:
            t = pltpu.roll(t, HEAD_DIM, 1)
        parts.append(jnp.where(mine, t, 0.0))
    qs = jnp.concatenate(parts, axis=0).astype(BF16)
    s = lax.dot_general(qs, k_all, (((1,), (1,)), ((), ())), preferred_element_type=F32)
    if mask is not None:
        s = jnp.where(mask, s, NEG_INF)
    blk = lax.broadcasted_iota(jnp.int32, (Q_PER_KV * rows, 1), 0)
    sink_col = jnp.zeros((Q_PER_KV * rows, 1), F32)
    for j in range(Q_PER_KV):
        hd = g * Q_PER_KV + j
        sink_col = jnp.where((blk >= j * rows) & (blk < (j + 1) * rows), sink[:, hd:hd + 1], sink_col)
    mx = jnp.maximum(jnp.max(s, axis=-1, keepdims=True), sink_col)
    p = jnp.exp(s - mx)
    den = jnp.sum(p, axis=-1, keepdims=True) + jnp.exp(sink_col - mx)
    o = _dot(p.astype(BF16), v_all)
    return o * (1.0 / den)


def _store_heads(att_s, r0, rows, outs):
    lane = lax.broadcasted_iota(jnp.int32, (rows, LANES), 1)
    low = lane < HEAD_DIM
    for t in range(ATT_HEADS // 2):
        g = (2 * t) // Q_PER_KV
        j = 2 * t - g * Q_PER_KV
        even = outs[g][j * rows:(j + 1) * rows, :]
        odd = outs[g][(j + 1) * rows:(j + 2) * rows, :]
        if g == 1:
            even = pltpu.roll(even, HEAD_DIM, 1)
        else:
            odd = pltpu.roll(odd, HEAD_DIM, 1)
        att_s[r0:r0 + rows, t * LANES:(t + 1) * LANES] = jnp.where(low, even, odd)


def _mixer_kernel(ly_ref, x_ref, mod_ref, w_ref, b_ref, cos_ref, sin_ref, sink_ref,
                  kc_ref, vc_ref, kp_ref, vp_ref, kt_ref, vt_ref, kn_ref, vn_ref,
                  hf_ref, hb_ref, sg_ref, sb_ref, ws_ref, bs_ref,
                  wa_ref, wb_ref, wc_ref, wo_ref, bo_ref, l1g_ref, l1b_ref, wrt_ref, brt_ref,
                  x1_ref, h2_ref, ri_ref, rf_ref, cnt_ref,
                  att_s, run_s, *, n_tiles):
    del ly_ref
    b = pl.program_id(0)
    i = pl.program_id(1)
    x = x_ref[...]
    m = mod_ref[...]
    h = (x * (1.0 + m[1:2, :]) + m[0:1, :]).astype(BF16)

    def proj(c0, width):
        return _dot(h, w_ref[:, c0:c0 + width]) + b_ref[:, c0:c0 + width]

    q = _rope(proj(Q0, ATT_W), cos_ref[...], sin_ref[...]) * (HEAD_DIM ** -0.5)
    sink = sink_ref[...]

    @pl.when(i == 0)
    def _():
        outs = [_attn_group(q, g, kc_ref[...], vc_ref[...], None, sink) for g in range(KV_HEADS)]
        _store_heads(att_s, 0, TILE, outs)

    @pl.when(i > 0)
    def _():
        n_keys = CTX_LEN + 3 * WINDOW
        r = lax.broadcasted_iota(jnp.int32, (Q_PER_KV * WINDOW, n_keys), 0) & (WINDOW - 1)
        c = lax.broadcasted_iota(jnp.int32, (Q_PER_KV * WINDOW, n_keys), 1)
        j_prev = c - CTX_LEN
        j_next = c - (CTX_LEN + 2 * WINDOW)
        base = (c < CTX_LEN) | ((c >= CTX_LEN + WINDOW) & (j_next < 0))
        prev_ok = (j_prev >= r) & (c >= CTX_LEN) & (c < CTX_LEN + WINDOW)
        next_ok = (j_next >= 0) & (j_next <= r)
        has_prev = jnp.broadcast_to(i > 1, prev_ok.shape)
        has_next = jnp.broadcast_to(i < n_tiles - 1, next_ok.shape)
        kt, vt = kt_ref[...], vt_ref[...]
        for half in range(2):
            if half == 0:
                k_all = jnp.concatenate([kc_ref[...], kp_ref[...], kt], axis=0)
                v_all = jnp.concatenate([vc_ref[...], vp_ref[...], vt], axis=0)
                mask = base | (prev_ok & has_prev) | next_ok
            else:
                k_all = jnp.concatenate([kc_ref[...], kt, kn_ref[...]], axis=0)
                v_all = jnp.concatenate([vc_ref[...], vt, vn_ref[...]], axis=0)
                mask = base | prev_ok | (next_ok & has_next)
            qh = q[half * WINDOW:(half + 1) * WINDOW, :]
            outs = [_attn_group(qh, g, k_all, v_all, mask, sink) for g in range(KV_HEADS)]
            _store_heads(att_s, half * WINDOW, WINDOW, outs)

    u = _gelu(proj(U0, SGU_W))
    vn = _layer_norm(_gelu(proj(V0, SGU_W)), sg_ref[...], sb_ref[...]).astype(BF16)
    mixed = []
    for c in range(TILE // CHUNK):
        row = []
        for g in range(SGU_GROUPS):
            row.append(_dot(ws_ref[g], vn[c * CHUNK:(c + 1) * CHUNK, g * CHUNK:(g + 1) * CHUNK]))
        mixed.append(jnp.concatenate(row, axis=1) + bs_ref[...])
    a_br = u * jnp.concatenate(mixed, axis=0)

    r_br = (hf_ref[...] + hb_ref[...]) * _gelu(proj(Z0, LRU_W))

    y = _sigmoid(proj(G0, D_MODEL)) * _dot(a_br.astype(BF16), wa_ref[...])
    y = y + _sigmoid(proj(G0 + D_MODEL, D_MODEL)) * _dot(att_s[...].astype(BF16), wb_ref[...])
    y = y + _sigmoid(proj(G0 + 2 * D_MODEL, D_MODEL)) * _dot(r_br.astype(BF16), wc_ref[...])
    out = _dot(y.astype(BF16), wo_ref[...]) + bo_ref[...]
    x1 = _layer_norm(ALPHA * x + m[2:3, :] * out, l1g_ref[...], l1b_ref[...])
    x1_ref[...] = x1
    h2 = x1 * (1.0 + m[4:5, :]) + m[3:4, :]
    h2_ref[...] = h2

    logits = _dot3(h2, wrt_ref[...]) + brt_ref[...]
    lane = lax.broadcasted_iota(jnp.int32, (TILE, LANES), 1)
    lane_f = lane.astype(F32)
    big = float(LANES)
    gl = jnp.where(lane < N_GROUPS, logits, NEG_INF)
    gmax = jnp.max(gl, axis=-1, keepdims=True)
    g_idx = jnp.min(jnp.where(gl == gmax, lane_f, big), axis=-1, keepdims=True)
    g_w = 1.0 / jnp.sum(jnp.exp(gl - gmax), axis=-1, keepdims=True)
    lo = N_GROUPS + EXPERTS_PER_GROUP * g_idx
    el = jnp.where((lane_f >= lo) & (lane_f < lo + EXPERTS_PER_GROUP), logits, NEG_INF)
    v1 = jnp.max(el, axis=-1, keepdims=True)
    l1 = jnp.min(jnp.where(el == v1, lane_f, big), axis=-1, keepdims=True)
    el2 = jnp.where(lane_f == l1, NEG_INF, el)
    v2 = jnp.max(el2, axis=-1, keepdims=True)
    l2 = jnp.min(jnp.where(el2 == v2, lane_f, big), axis=-1, keepdims=True)
    e21 = jnp.exp(v2 - v1)
    gate1 = g_w / (1.0 + e21)
    gate2 = g_w * e21 / (1.0 + e21)
    e1 = l1 - N_GROUPS
    e2 = l2 - N_GROUPS

    @pl.when((b == 0) & (i == 0))
    def _():
        run_s[...] = jnp.zeros_like(run_s)

    oh1 = jnp.where(lane_f == e1, 1.0, 0.0)
    oh2 = jnp.where(lane_f == e2, 1.0, 0.0)
    both = oh1 + oh2
    tr = lax.broadcasted_iota(jnp.int32, (TILE, TILE), 0)
    tc = lax.broadcasted_iota(jnp.int32, (TILE, TILE), 1)
    earlier = jnp.where(tc < tr, 1.0, 0.0).astype(BF16)
    before = _dot(earlier, both.astype(BF16)) + run_s[...]
    rank1 = jnp.sum(before * oh1, axis=-1, keepdims=True)
    rank2 = jnp.sum(before * oh2, axis=-1, keepdims=True)
    run_s[...] = run_s[...] + jnp.sum(both, axis=0, keepdims=True)
    cnt_ref[...] = jnp.broadcast_to(run_s[...], cnt_ref.shape)

    ri = jnp.where(lane == 0, e1, jnp.where(lane == 1, e2, jnp.where(lane == 2, rank1,
                                                                      jnp.where(lane == 3, rank2, 0.0))))
    ri_ref[...] = ri.astype(jnp.int32)
    rf_ref[...] = jnp.where(lane == 0, gate1, jnp.where(lane == 1, gate2, 0.0))


def _mixer(lyr, xc, mods, w_main, b_main, cos_t, sin_t, sink, k, v, hf, hb,
           sg, sb, w_sp, b_sp, wa, wb, wc, wo, bo, l1g, l1b, w_rt, b_rt):
    bsz, n_tok, _ = xc.shape
    nt = n_tok // TILE
    n_half = n_tok // WINDOW
    once = pl.Buffered(1)
    tile = lambda b, i, ly: (b, i, 0)
    lay3 = lambda b, i, ly: (ly[0], 0, 0)
    flat = lambda b, i, ly: (b * nt + i, 0)
    kv_ctx = pl.BlockSpec((None, CTX_LEN, KV_W), lambda b, i, ly: (b, 0, 0))
    kv_prev = pl.BlockSpec((None, WINDOW, KV_W), lambda b, i, ly: (b, jnp.maximum(2 * i - 1, 0), 0))
    kv_this = pl.BlockSpec((None, TILE, KV_W), tile)
    kv_next = pl.BlockSpec((None, WINDOW, KV_W), lambda b, i, ly: (b, jnp.minimum(2 * i + 2, n_half - 1), 0))
    row_vec = lambda width: pl.BlockSpec((None, 1, width), lay3)
    return pl.pallas_call(
        functools.partial(_mixer_kernel, n_tiles=nt),
        out_shape=(jax.ShapeDtypeStruct((bsz, n_tok, D_MODEL), F32),
                   jax.ShapeDtypeStruct((bsz * n_tok, D_MODEL), F32),
                   jax.ShapeDtypeStruct((bsz * n_tok, LANES), jnp.int32),
                   jax.ShapeDtypeStruct((bsz * n_tok, LANES), F32),
                   jax.ShapeDtypeStruct((SUBLANES, LANES), F32)),
        grid_spec=pltpu.PrefetchScalarGridSpec(
            num_scalar_prefetch=1,
            grid=(bsz, nt),
            in_specs=[
                pl.BlockSpec((None, TILE, D_MODEL), tile),
                pl.BlockSpec((None, None, N_MOD, D_MODEL),
                             lambda b, i, ly: (ly[0], jnp.where(i == 0, bsz, b), 0, 0)),
                pl.BlockSpec((None, D_MODEL, MAIN_W), lay3, pipeline_mode=once),
                row_vec(MAIN_W),
                pl.BlockSpec((TILE, KV_W), lambda b, i, ly: (i, 0)),
                pl.BlockSpec((TILE, KV_W), lambda b, i, ly: (i, 0)),
                pl.BlockSpec((None, 1, ATT_HEADS), lay3),
                kv_ctx, kv_ctx, kv_prev, kv_prev, kv_this, kv_this, kv_next, kv_next,
                pl.BlockSpec((None, TILE, LRU_W), tile),
                pl.BlockSpec((None, TILE, LRU_W), tile),
                row_vec(SGU_W), row_vec(SGU_W),
                pl.BlockSpec((None, SGU_GROUPS, CHUNK, CHUNK), lambda b, i, ly: (ly[0], 0, 0, 0)),
                pl.BlockSpec((None, CHUNK, SGU_W), lay3),
                pl.BlockSpec((None, SGU_W, D_MODEL), lay3, pipeline_mode=once),
                pl.BlockSpec((None, ATT_W, D_MODEL), lay3, pipeline_mode=once),
                pl.BlockSpec((None, LRU_W, D_MODEL), lay3, pipeline_mode=once),
                pl.BlockSpec((None, D_MODEL, D_MODEL), lay3, pipeline_mode=once),
                row_vec(D_MODEL), row_vec(D_MODEL), row_vec(D_MODEL),
                pl.BlockSpec((None, D_MODEL, LANES), lay3),
                row_vec(LANES),
            ],
            out_specs=(pl.BlockSpec((None, TILE, D_MODEL), tile),
                       pl.BlockSpec((TILE, D_MODEL), flat),
                       pl.BlockSpec((TILE, LANES), flat),
                       pl.BlockSpec((TILE, LANES), flat),
                       pl.BlockSpec((SUBLANES, LANES), lambda b, i, ly: (0, 0))),
            scratch_shapes=[pltpu.VMEM((TILE, ATT_W), F32), pltpu.VMEM((1, LANES), F32)],
        ),
        compiler_params=pltpu.CompilerParams(dimension_semantics=("arbitrary", "arbitrary"),
                                             vmem_limit_bytes=VMEM_LIMIT),
        name="mixer",
    )(lyr, xc, mods, w_main, b_main, cos_t, sin_t, sink, k, v, k, v, k, v, k, v, hf, hb,
      sg, sb, w_sp, b_sp, wa, wb, wc, wo, bo, l1g, l1b, w_rt, b_rt)


def _dispatch_kernel(slot_ref, h2_ref, buf_in_ref, buf_ref, sem):
    del buf_in_ref
    base = pl.program_id(0) * TILE

    def row_copy(t, k):
        tok = base + t
        return pltpu.make_async_copy(h2_ref.at[pl.ds(tok, 1)],
                                     buf_ref.at[pl.ds(slot_ref[TOP_K * tok + k], 1)], sem)

    def start(t, carry):
        for k in range(TOP_K):
            row_copy(t, k).start()
        return carry

    def wait(t, carry):
        for k in range(TOP_K):
            row_copy(t, k).wait()
        return carry

    lax.fori_loop(0, TILE, start, 0, unroll=8)
    lax.fori_loop(0, TILE, wait, 0, unroll=8)


def _dispatch(slot, h2, n_slots):
    n_tok = h2.shape[0]
    return pl.pallas_call(
        _dispatch_kernel,
        out_shape=jax.ShapeDtypeStruct((n_slots, D_MODEL), F32),
        grid_spec=pltpu.PrefetchScalarGridSpec(
            num_scalar_prefetch=1,
            grid=(n_tok // TILE,),
            in_specs=[pl.BlockSpec(memory_space=pl.ANY), pl.BlockSpec(memory_space=pl.ANY)],
            out_specs=pl.BlockSpec(memory_space=pl.ANY),
            scratch_shapes=[pltpu.SemaphoreType.DMA(())],
        ),
        input_output_aliases={2: 0},
        compiler_params=pltpu.CompilerParams(dimension_semantics=("arbitrary",)),
        name="dispatch",
    )(slot, h2, jnp.zeros((n_slots, D_MODEL), F32))


def _experts_kernel(ly_ref, be_ref, nu_ref, x_ref, w1_ref, w3_ref, w2_ref, o_ref, w1s, w3s, w2s):
    del ly_ref
    i = pl.program_id(0)
    changed = (i == 0) | (be_ref[i] != be_ref[jnp.maximum(i - 1, 0)])

    @pl.when((i < nu_ref[0]) & changed)
    def _():
        w1s[...] = w1_ref[...].astype(BF16)
        w3s[...] = w3_ref[...].astype(BF16)
        w2s[...] = w2_ref[...].astype(BF16)

    @pl.when(i < nu_ref[0])
    def _():
        x = x_ref[...].astype(BF16)
        h1 = _dot(x, w1s[...])
        h3 = _dot(x, w3s[...])
        act = (h1 * _sigmoid(h1) * h3).astype(BF16)
        o_ref[...] = _dot(act, w2s[...])

    @pl.when(i >= nu_ref[0])
    def _():
        o_ref[...] = jnp.zeros_like(o_ref)


def _experts(lyr, block_expert, n_used, buf, w1, w3, w2):
    n_blocks = buf.shape[0] // EXPERT_BLOCK

    def blk(i, ly, be, nu):
        return (jnp.minimum(i, nu[0] - 1), 0)

    def wsel(i, ly, be, nu):
        return (ly[0], be[jnp.minimum(i, nu[0] - 1)], 0, 0)

    return pl.pallas_call(
        _experts_kernel,
        out_shape=jax.ShapeDtypeStruct(buf.shape, F32),
        grid_spec=pltpu.PrefetchScalarGridSpec(
            num_scalar_prefetch=3,
            grid=(n_blocks,),
            in_specs=[pl.BlockSpec((EXPERT_BLOCK, D_MODEL), blk),
                      pl.BlockSpec((None, None, D_MODEL, D_EXPERT), wsel),
                      pl.BlockSpec((None, None, D_MODEL, D_EXPERT), wsel),
                      pl.BlockSpec((None, None, D_EXPERT, D_MODEL), wsel)],
            out_specs=pl.BlockSpec((EXPERT_BLOCK, D_MODEL), lambda i, ly, be, nu: (i, 0)),
            scratch_shapes=[pltpu.VMEM((D_MODEL, D_EXPERT), BF16),
                            pltpu.VMEM((D_MODEL, D_EXPERT), BF16),
                            pltpu.VMEM((D_EXPERT, D_MODEL), BF16)],
        ),
        compiler_params=pltpu.CompilerParams(dimension_semantics=("arbitrary",),
                                             vmem_limit_bytes=40 * MIB),
        name="experts",
    )(lyr, block_expert, n_used, buf, w1, w3, w2)


def _combine_kernel(ly_ref, slot_ref, x1_ref, mod_ref, rf_ref, g_ref, b_ref, eo_ref, o_ref, gbuf, sems,
                    *, n_steps):
    del ly_ref
    s = pl.program_id(0)

    def row_copy(step, t, k):
        par = step % 2
        src = eo_ref.at[pl.ds(slot_ref[TOP_K * (step * TILE + t) + k], 1)]
        return pltpu.make_async_copy(src, gbuf.at[par, k, pl.ds(t, 1)], sems.at[par])

    def start_all(step):
        def body(t, carry):
            for k in range(TOP_K):
                row_copy(step, t, k).start()
            return carry
        lax.fori_loop(0, TILE, body, 0, unroll=8)

    @pl.when(s == 0)
    def _():
        start_all(s)

    @pl.when(s + 1 < n_steps)
    def _():
        start_all(s + 1)

    def wait_body(t, carry):
        for k in range(TOP_K):
            row_copy(s, t, k).wait()
        return carry
    lax.fori_loop(0, TILE, wait_body, 0, unroll=8)

    par = s % 2
    rf = rf_ref[...]
    y = rf[:, 0:1] * gbuf[par, 0] + rf[:, 1:2] * gbuf[par, 1]
    m = mod_ref[...]
    o_ref[...] = _layer_norm(ALPHA * x1_ref[...] + m[5:6, :] * y, g_ref[...], b_ref[...])


def _combine(lyr, slot, x1, mods, rf, l2g, l2b, eo):
    bsz, n_tok, _ = x1.shape
    nt = n_tok // TILE
    n_steps = bsz * nt
    tile = lambda s, ly, sl: (s // nt, s % nt, 0)
    lay3 = lambda s, ly, sl: (ly[0], 0, 0)
    return pl.pallas_call(
        functools.partial(_combine_kernel, n_steps=n_steps),
        out_shape=jax.ShapeDtypeStruct((bsz, n_tok, D_MODEL), F32),
        grid_spec=pltpu.PrefetchScalarGridSpec(
            num_scalar_prefetch=2,
            grid=(n_steps,),
            in_specs=[pl.BlockSpec((None, TILE, D_MODEL), tile),
                      pl.BlockSpec((None, None, N_MOD, D_MODEL),
                                   lambda s, ly, sl: (ly[0], jnp.where(s % nt == 0, bsz, s // nt), 0, 0)),
                      pl.BlockSpec((TILE, LANES), lambda s, ly, sl: (s, 0)),
                      pl.BlockSpec((None, 1, D_MODEL), lay3),
                      pl.BlockSpec((None, 1, D_MODEL), lay3),
                      pl.BlockSpec(memory_space=pl.ANY)],
            out_specs=pl.BlockSpec((None, TILE, D_MODEL), tile),
            scratch_shapes=[pltpu.VMEM((2, TOP_K, TILE, D_MODEL), F32),
                            pltpu.SemaphoreType.DMA((2,))],
        ),
        compiler_params=pltpu.CompilerParams(dimension_semantics=("arbitrary",),
                                             vmem_limit_bytes=40 * MIB),
        name="combine",
    )(lyr, slot, x1, mods, rf, l2g, l2b, eo)


def _slot_tables(route_i, counts, n_blocks):
    counts = counts.astype(jnp.int32)
    padded = (counts + EXPERT_BLOCK - 1) // EXPERT_BLOCK * EXPERT_BLOCK
    pad_ends = jnp.cumsum(padded)
    pad_starts = pad_ends - padded
    expert = route_i[:, :TOP_K]
    rank = route_i[:, TOP_K:2 * TOP_K]
    slot = (pad_starts[expert] + rank).reshape(-1)
    block_expert = jnp.minimum(
        jnp.searchsorted(pad_ends, jnp.arange(n_blocks, dtype=jnp.int32) * EXPERT_BLOCK, side='right'),
        N_EXPERTS - 1).astype(jnp.int32)
    n_used = (pad_ends[-1:] // EXPERT_BLOCK).astype(jnp.int32)
    return slot, block_expert, n_used


def _rope_tables(n_lat):
    rows = n_lat // GRID_W
    row = jnp.repeat(jnp.arange(rows, dtype=F32), GRID_W)
    col = jnp.tile(jnp.arange(GRID_W, dtype=F32), rows)
    axis_dim = HEAD_DIM // 2
    inv_freq = ROPE_THETA ** (-jnp.arange(0, axis_dim, 2, dtype=F32) / axis_dim)
    ang_r = row[:, None] * inv_freq
    ang_c = col[:, None] * inv_freq
    cos = jnp.concatenate([jnp.cos(ang_r)] * 2 + [jnp.cos(ang_c)] * 2, axis=1)
    sin = jnp.concatenate([-jnp.sin(ang_r), jnp.sin(ang_r), -jnp.sin(ang_c), jnp.sin(ang_c)], axis=1)
    cos = jnp.concatenate([jnp.ones((CTX_LEN, HEAD_DIM), F32), cos], axis=0)
    sin = jnp.concatenate([jnp.zeros((CTX_LEN, HEAD_DIM), F32), sin], axis=0)
    return jnp.tile(cos, (1, KV_HEADS)), jnp.tile(sin, (1, KV_HEADS))


def kernel(x, c, ctx, c_ctx, w_mod, b_mod, w_in, b_in, sgu_ln_g, sgu_ln_b, w_spatial, b_spatial, attn_sink, conv_w, conv_b, w_rgate, b_rgate, w_igate, b_igate, lru_lambda, w_proj_a, w_proj_b, w_proj_c, w_out, b_out, ln1_g, ln1_b, ln2_g, ln2_b, w_group, b_group, w_router, b_router, w1, w3, w2):
    bsz, n_lat, _ = x.shape
    n_layers = w_mod.shape[0]
    assert ctx.shape[1] == CTX_LEN == TILE and n_lat % TILE == 0 and bsz < SUBLANES
    n_tok = CTX_LEN + n_lat
    n_blocks = -(-bsz * n_tok * TOP_K // EXPERT_BLOCK) + N_EXPERTS
    n_slots = n_blocks * EXPERT_BLOCK

    cond = jnp.zeros((SUBLANES, D_MODEL), F32).at[:bsz].set(c).at[bsz].set(c_ctx)
    mods = _modulation(cond, w_mod, b_mod).reshape(n_layers, SUBLANES, N_MOD, D_MODEL)
    w_main = w_in[:, :, :MAIN_W].astype(BF16)
    w_ctx = w_in[:, :, MAIN_W:].astype(BF16)
    b_main = b_in[:, None, :MAIN_W]
    b_ctx = b_in[:, None, MAIN_W:]
    w_ri = jnp.concatenate([w_rgate, w_igate], axis=-1).astype(BF16)
    w_sp = w_spatial.astype(BF16)
    b_sp = jnp.repeat(jnp.swapaxes(b_spatial, 1, 2), CHUNK, axis=2)
    wa, wb, wc, wo = (w.astype(BF16) for w in (w_proj_a, w_proj_b, w_proj_c, w_out))
    w_rt = jnp.zeros((n_layers, D_MODEL, LANES), F32)
    w_rt = w_rt.at[:, :, :N_GROUPS].set(w_group).at[:, :, N_GROUPS:N_GROUPS + N_EXPERTS].set(w_router)
    b_rt = jnp.zeros((n_layers, 1, LANES), F32)
    b_rt = b_rt.at[:, 0, :N_GROUPS].set(b_group).at[:, 0, N_GROUPS:N_GROUPS + N_EXPERTS].set(b_router)
    cos_t, sin_t = _rope_tables(n_lat)
    row3 = lambda a: a[:, None, :]
    conv_b4, b_r4, b_i4 = (a[:, :, None, :] for a in (conv_b, b_rgate, b_igate))
    sink, sg, sb, bo = row3(attn_sink), row3(sgu_ln_g), row3(sgu_ln_b), row3(b_out)
    l1g, l1b, l2g, l2b = row3(ln1_g), row3(ln1_b), row3(ln2_g), row3(ln2_b)

    def layer(l, xc):
        lyr = jnp.full((1,), l, jnp.int32)
        k, v, hf, hb = _kv_lru(lyr, xc, mods, w_ctx, b_ctx, cos_t, sin_t, conv_w, conv_b4,
                               w_ri, b_r4, b_i4, lru_lambda)
        x1, h2, route_i, route_f, counts = _mixer(
            lyr, xc, mods, w_main, b_main, cos_t, sin_t, sink, k, v, hf, hb,
            sg, sb, w_sp, b_sp, wa, wb, wc, wo, bo, l1g, l1b, w_rt, b_rt)
        slot, block_expert, n_used = _slot_tables(route_i, counts[0, :N_EXPERTS], n_blocks)
        buf = _dispatch(slot, h2, n_slots)
        eo = _experts(lyr, block_expert, n_used, buf, w1, w3, w2)
        return _combine(lyr, slot, x1, mods, route_f, l2g, l2b, eo)

    xc = jnp.concatenate([ctx, x], axis=1)
    for l in range(n_layers):
        xc = layer(l, xc)
    return xc[:, CTX_LEN:]
```

```python
import functools

import jax
import jax.numpy as jnp
from jax import lax
from jax.experimental import pallas as pl
from jax.experimental.pallas import tpu as pltpu

F32 = jnp.float32
BF16 = jnp.bfloat16

D_MODEL = 1024
DEPTH = 4
GRID_W = 64
CTX_LEN = 256
HEAD_DIM = 64
ATT_HEADS = 8
KV_HEADS = 2
Q_PER_KV = ATT_HEADS // KV_HEADS
ATT_W = ATT_HEADS * HEAD_DIM
KV_W = KV_HEADS * HEAD_DIM
WINDOW = 128
ROPE_THETA = 10000.0
CHUNK = 128
SGU_GROUPS = 4
SGU_W = SGU_GROUPS * CHUNK
LRU_W = D_MODEL
LRU_BLOCKS = 8
LRU_BW = LRU_W // LRU_BLOCKS
CONV_W = 4
LRU_C = 8.0
N_GROUPS = 4
EXPERTS_PER_GROUP = 8
N_EXPERTS = N_GROUPS * EXPERTS_PER_GROUP
TOP_K = 2
D_EXPERT = 512
N_MOD = 6
ALPHA = (2.0 * DEPTH) ** 0.25
LN_EPS = 1e-6
NEG_INF = -1e30

Q0, U0, V0, G0, Z0 = 0, ATT_W, ATT_W + SGU_W, ATT_W + 2 * SGU_W, ATT_W + 2 * SGU_W + 3 * D_MODEL
MAIN_W = Z0 + LRU_W
CTX_W = 2 * KV_W + LRU_W

TILE = 256
SUBLANES = 8
LANES = 128
EXPERT_BLOCK = 256
MIB = 1024 * 1024
VMEM_LIMIT = 56 * MIB


def _sigmoid(x):
    return 0.5 * (1.0 + jnp.tanh(0.5 * x))


def _gelu(x):
    return 0.5 * x * (1.0 + jnp.tanh(0.7978845608028654 * (x + 0.044715 * (x * x * x))))


def _layer_norm(x, g, b):
    mu = jnp.mean(x, axis=-1, keepdims=True)
    xc = x - mu
    var = jnp.mean(xc * xc, axis=-1, keepdims=True)
    return xc * lax.rsqrt(var + LN_EPS) * g + b


def _dot(a, b):
    return jnp.dot(a, b, preferred_element_type=F32)


def _dot3(a, b):
    a_hi = a.astype(BF16)
    a_lo = (a - a_hi.astype(F32)).astype(BF16)
    b_hi = b.astype(BF16)
    b_lo = (b - b_hi.astype(F32)).astype(BF16)
    return _dot(a_hi, b_hi) + (_dot(a_hi, b_lo) + _dot(a_lo, b_hi))


def _rope(x, cos, sin_signed):
    n = x.shape[1] // LANES
    if n > 1:
        cos = jnp.concatenate([cos] * n, axis=1)
        sin_signed = jnp.concatenate([sin_signed] * n, axis=1)
    lane = lax.broadcasted_iota(jnp.int32, x.shape, 1)
    first = (lane & 31) < 16
    w = x.shape[1]
    partner = jnp.where(first, pltpu.roll(x, w - 16, 1), pltpu.roll(x, 16, 1))
    return x * cos + partner * sin_signed


def _mods_kernel(c_ref, w_ref, b_ref, o_ref):
    c = c_ref[...]
    o_ref[...] = _dot3(c * _sigmoid(c), w_ref[...]) + b_ref[...]


def _modulation(cond, w_mod, b_mod):
    n_layers = w_mod.shape[0]
    tn = 1536
    return pl.pallas_call(
        _mods_kernel,
        out_shape=jax.ShapeDtypeStruct((n_layers, SUBLANES, N_MOD * D_MODEL), F32),
        grid=(n_layers, N_MOD * D_MODEL // tn),
        in_specs=[
            pl.BlockSpec((SUBLANES, D_MODEL), lambda l, j: (0, 0)),
            pl.BlockSpec((None, D_MODEL, tn), lambda l, j: (l, 0, j)),
            pl.BlockSpec((None, 1, tn), lambda l, j: (l, 0, j)),
        ],
        out_specs=pl.BlockSpec((None, SUBLANES, tn), lambda l, j: (l, 0, j)),
        compiler_params=pltpu.CompilerParams(dimension_semantics=("arbitrary", "arbitrary"),
                                             vmem_limit_bytes=40 * MIB),
        name="modulation",
    )(cond, w_mod, b_mod.reshape(n_layers, 1, N_MOD * D_MODEL))


def _scan_tile(a_ref, u_ref, carry_ref, out_ref, reverse):
    width = a_ref.shape[1]
    row = lax.broadcasted_iota(jnp.int32, (SUBLANES, width), 0)
    n_blk = a_ref.shape[0] // SUBLANES

    def body(j, carry):
        blk = (n_blk - 1 - j) if reverse else j
        r0 = pl.multiple_of(blk * SUBLANES, SUBLANES)
        a = a_ref[pl.ds(r0, SUBLANES), :]
        u = u_ref[pl.ds(r0, SUBLANES), :]
        for s in (1, 2, 4):
            if reverse:
                keep = row < SUBLANES - s
                a_sh = jnp.where(keep, pltpu.roll(a, SUBLANES - s, 0), 1.0)
                u_sh = jnp.where(keep, pltpu.roll(u, SUBLANES - s, 0), 0.0)
            else:
                keep = row >= s
                a_sh = jnp.where(keep, pltpu.roll(a, s, 0), 1.0)
                u_sh = jnp.where(keep, pltpu.roll(u, s, 0), 0.0)
            u = a * u_sh + u
            a = a * a_sh
        h = u + a * carry
        out_ref[pl.ds(r0, SUBLANES), :] = h
        return h[0:1, :] if reverse else h[SUBLANES - 1:SUBLANES, :]

    carry_ref[...] = lax.fori_loop(0, n_blk, body, carry_ref[...])


def _lru_inputs(t, wri_ref, br_ref, bi_ref, sp, a_ref, u_ref):
    for n in range(LRU_BLOCKS):
        cols = slice(n * LRU_BW, (n + 1) * LRU_BW)
        tn = t[:, cols]
        g = _dot(tn.astype(BF16), wri_ref[n])
        r = _sigmoid(g[:, :LRU_BW] + br_ref[:, cols])
        i = _sigmoid(g[:, LRU_BW:] + bi_ref[:, cols])
        a = jnp.exp((-LRU_C) * r * sp[:, cols])
        a_ref[:, cols] = a
        u_ref[:, cols] = jnp.sqrt(1.0 - a * a) * i * tn


def _kv_lru_kernel(ly_ref, xf_ref, xb_ref, mod_ref, wc_ref, bc_ref, cos_ref, sin_ref,
                   cw_ref, cb_ref, wri_ref, br_ref, bi_ref, lam_ref,
                   k_ref, v_ref, hf_ref, hb_ref,
                   xsf, xsb, af, uf, ab, ub, cf, cbk):
    del ly_ref
    i = pl.program_id(1)
    m = mod_ref[...]
    shift, scale = m[0:1, :], 1.0 + m[1:2, :]

    @pl.when(i == 0)
    def _():
        cf[...] = jnp.zeros_like(cf)
        cbk[...] = jnp.zeros_like(cbk)

    h = (xf_ref[...] * scale + shift).astype(BF16)
    p = _dot(h, wc_ref[...]) + bc_ref[...]
    k_ref[...] = _rope(p[:, :KV_W], cos_ref[...], sin_ref[...]).astype(BF16)
    v_ref[...] = p[:, KV_W:2 * KV_W].astype(BF16)
    xr_f = p[:, 2 * KV_W:]
    h = (xb_ref[...] * scale + shift).astype(BF16)
    xr_b = _dot(h, wc_ref[:, 2 * KV_W:]) + bc_ref[:, 2 * KV_W:]

    @pl.when(i <= 1)
    def _():
        xsf[0:SUBLANES, :] = jnp.zeros((SUBLANES, LRU_W), F32)
        xsb[TILE:TILE + SUBLANES, :] = jnp.zeros((SUBLANES, LRU_W), F32)

    @pl.when(i > 1)
    def _():
        xsf[0:SUBLANES, :] = xsf[TILE:TILE + SUBLANES, :]
        xsb[TILE:TILE + SUBLANES, :] = xsb[0:SUBLANES, :]

    xsf[SUBLANES:SUBLANES + TILE, :] = xr_f
    xsb[0:TILE, :] = xr_b

    lam = lam_ref[...]
    sp = jnp.maximum(-lam, 0.0) + jnp.log1p(jnp.exp(-jnp.abs(lam)))

    cw = cw_ref[0]
    t = (cb_ref[0] + cw[0:1, :] * xsf[SUBLANES - 3:SUBLANES - 3 + TILE, :]
         + cw[1:2, :] * xsf[SUBLANES - 2:SUBLANES - 2 + TILE, :]
         + cw[2:3, :] * xsf[SUBLANES - 1:SUBLANES - 1 + TILE, :]
         + cw[3:4, :] * xr_f)
    _lru_inputs(t, wri_ref.at[0], br_ref.at[0], bi_ref.at[0], sp[0:1, :], af, uf)
    cw = cw_ref[1]
    t = (cb_ref[1] + cw[0:1, :] * xr_b
         + cw[1:2, :] * xsb[1:1 + TILE, :]
         + cw[2:3, :] * xsb[2:2 + TILE, :]
         + cw[3:4, :] * xsb[3:3 + TILE, :])
    _lru_inputs(t, wri_ref.at[1], br_ref.at[1], bi_ref.at[1], sp[1:2, :], ab, ub)

    _scan_tile(af, uf, cf, hf_ref, reverse=False)
    _scan_tile(ab, ub, cbk, hb_ref, reverse=True)


def _kv_lru(lyr, xc, mods, w_ctx, b_ctx, cos_t, sin_t, conv_w, conv_b, w_ri, b_r, b_i, lam):
    bsz, n_tok, _ = xc.shape
    nt = n_tok // TILE
    tile_f = lambda b, i, ly: (b, i, 0)
    tile_b = lambda b, i, ly: (b, jnp.where(i == 0, 0, nt - i), 0)
    lay3 = lambda b, i, ly: (ly[0], 0, 0)
    lay4 = lambda b, i, ly: (ly[0], 0, 0, 0)
    return pl.pallas_call(
        _kv_lru_kernel,
        out_shape=(jax.ShapeDtypeStruct((bsz, n_tok, KV_W), BF16),
                   jax.ShapeDtypeStruct((bsz, n_tok, KV_W), BF16),
                   jax.ShapeDtypeStruct((bsz, n_tok, LRU_W), F32),
                   jax.ShapeDtypeStruct((bsz, n_tok, LRU_W), F32)),
        grid_spec=pltpu.PrefetchScalarGridSpec(
            num_scalar_prefetch=1,
            grid=(bsz, nt),
            in_specs=[
                pl.BlockSpec((None, TILE, D_MODEL), tile_f),
                pl.BlockSpec((None, TILE, D_MODEL), tile_b),
                pl.BlockSpec((None, None, N_MOD, D_MODEL),
                             lambda b, i, ly: (ly[0], jnp.where(i == 0, bsz, b), 0, 0)),
                pl.BlockSpec((None, D_MODEL, CTX_W), lay3),
                pl.BlockSpec((None, 1, CTX_W), lay3),
                pl.BlockSpec((TILE, KV_W), lambda b, i, ly: (i, 0)),
                pl.BlockSpec((TILE, KV_W), lambda b, i, ly: (i, 0)),
                pl.BlockSpec((None, 2, CONV_W, LRU_W), lay4),
                pl.BlockSpec((None, 2, 1, LRU_W), lay4),
                pl.BlockSpec((None, 2, LRU_BLOCKS, LRU_BW, 2 * LRU_BW), lambda b, i, ly: (ly[0], 0, 0, 0, 0)),
                pl.BlockSpec((None, 2, 1, LRU_W), lay4),
                pl.BlockSpec((None, 2, 1, LRU_W), lay4),
                pl.BlockSpec((None, 2, LRU_W), lay3),
            ],
            out_specs=(pl.BlockSpec((None, TILE, KV_W), tile_f),
                       pl.BlockSpec((None, TILE, KV_W), tile_f),
                       pl.BlockSpec((None, TILE, LRU_W), tile_f),
                       pl.BlockSpec((None, TILE, LRU_W), tile_b)),
            scratch_shapes=[pltpu.VMEM((TILE + SUBLANES, LRU_W), F32),
                            pltpu.VMEM((TILE + SUBLANES, LRU_W), F32),
                            pltpu.VMEM((TILE, LRU_W), F32), pltpu.VMEM((TILE, LRU_W), F32),
                            pltpu.VMEM((TILE, LRU_W), F32), pltpu.VMEM((TILE, LRU_W), F32),
                            pltpu.VMEM((1, LRU_W), F32), pltpu.VMEM((1, LRU_W), F32)],
        ),
        compiler_params=pltpu.CompilerParams(dimension_semantics=("arbitrary", "arbitrary"),
                                             vmem_limit_bytes=VMEM_LIMIT),
        name="kv_lru",
    )(lyr, xc, xc, mods, w_ctx, b_ctx, cos_t, sin_t, conv_w, conv_b, w_ri, b_r, b_i, lam)


def _attn_group(q, g, k_all, v_all, mask, sink):
    rows = q.shape[0]
    lane = lax.broadcasted_iota(jnp.int32, (rows, LANES), 1)
    mine = (lane >= g * HEAD_DIM) & (lane < (g + 1) * HEAD_DIM)
    parts = []
    for j in range(Q_PER_KV):
        hd = g * Q_PER_KV + j
        t = q[:, (hd // 2) * LANES:(hd // 2 + 1) * LANES]
        if hd % 2 != g:
            t = pltpu.roll(t, HEAD_DIM, 1)
        parts.append(jnp.where(mine, t, 0.0))
    qs = jnp.concatenate(parts, axis=0).astype(BF16)
    s = lax.dot_general(qs, k_all, (((1,), (1,)), ((), ())), preferred_element_type=F32)
    if mask is not None:
        s = jnp.where(mask, s, NEG_INF)
    blk = lax.broadcasted_iota(jnp.int32, (Q_PER_KV * rows, 1), 0)
    sink_col = jnp.zeros((Q_PER_KV * rows, 1), F32)
    for j in range(Q_PER_KV):
        hd = g * Q_PER_KV + j
        sink_col = jnp.where((blk >= j * rows) & (blk < (j + 1) * rows), sink[:, hd:hd + 1], sink_col)
    mx = jnp.maximum(jnp.max(s, axis=-1, keepdims=True), sink_col)
    p = jnp.exp(s - mx)
    den = jnp.sum(p, axis=-1, keepdims=True) + jnp.exp(sink_col - mx)
    o = _dot(p.astype(BF16), v_all)
    return o * (1.0 / den)


def _store_heads(att_s, r0, rows, outs):
    lane = lax.broadcasted_iota(jnp.int32, (rows, LANES), 1)
    low = lane < HEAD_DIM
    for t in range(ATT_HEADS // 2):
        g = (2 * t) // Q_PER_KV
        j = 2 * t - g * Q_PER_KV
        even = outs[g][j * rows:(j + 1) * rows, :]
        odd = outs[g][(j + 1) * rows:(j + 2) * rows, :]
        if g == 1:
            even = pltpu.roll(even, HEAD_DIM, 1)
        else:
            odd = pltpu.roll(odd, HEAD_DIM, 1)
        att_s[r0:r0 + rows, t * LANES:(t + 1) * LANES] = jnp.where(low, even, odd)


def _mixer_kernel(ly_ref, x_ref, mod_ref, w_ref, b_ref, cos_ref, sin_ref, sink_ref,
                  kc_ref, vc_ref, kp_ref, vp_ref, kt_ref, vt_ref, kn_ref, vn_ref,
                  hf_ref, hb_ref, sg_ref, sb_ref, ws_ref, bs_ref,
                  wa_ref, wb_ref, wc_ref, wo_ref, bo_ref, l1g_ref, l1b_ref, wrt_ref, brt_ref,
                  x1_ref, h2_ref, ri_ref, rf_ref, cnt_ref,
                  att_s, run_s, *, n_tiles):
    del ly_ref
    b = pl.program_id(0)
    i = pl.program_id(1)
    x = x_ref[...]
    m = mod_ref[...]
    h = (x * (1.0 + m[1:2, :]) + m[0:1, :]).astype(BF16)

    def proj(c0, width):
        return _dot(h, w_ref[:, c0:c0 + width]) + b_ref[:, c0:c0 + width]

    q = _rope(proj(Q0, ATT_W), cos_ref[...], sin_ref[...]) * (HEAD_DIM ** -0.5)
    sink = sink_ref[...]

    @pl.when(i == 0)
    def _():
        outs = [_attn_group(q, g, kc_ref[...], vc_ref[...], None, sink) for g in range(KV_HEADS)]
        _store_heads(att_s, 0, TILE, outs)

    @pl.when(i > 0)
    def _():
        n_keys = CTX_LEN + 3 * WINDOW
        r = lax.broadcasted_iota(jnp.int32, (Q_PER_KV * WINDOW, n_keys), 0) & (WINDOW - 1)
        c = lax.broadcasted_iota(jnp.int32, (Q_PER_KV * WINDOW, n_keys), 1)
        j_prev = c - CTX_LEN
        j_next = c - (CTX_LEN + 2 * WINDOW)
        base = (c < CTX_LEN) | ((c >= CTX_LEN + WINDOW) & (j_next < 0))
        prev_ok = (j_prev >= r) & (c >= CTX_LEN) & (c < CTX_LEN + WINDOW)
        next_ok = (j_next >= 0) & (j_next <= r)
        has_prev = jnp.broadcast_to(i > 1, prev_ok.shape)
        has_next = jnp.broadcast_to(i < n_tiles - 1, next_ok.shape)
        kt, vt = kt_ref[...], vt_ref[...]
        for half in range(2):
            if half == 0:
                k_all = jnp.concatenate([kc_ref[...], kp_ref[...], kt], axis=0)
                v_all = jnp.concatenate([vc_ref[...], vp_ref[...], vt], axis=0)
                mask = base | (prev_ok & has_prev) | next_ok
            else:
                k_all = jnp.concatenate([kc_ref[...], kt, kn_ref[...]], axis=0)
                v_all = jnp.concatenate([vc_ref[...], vt, vn_ref[...]], axis=0)
                mask = base | prev_ok | (next_ok & has_next)
            qh = q[half * WINDOW:(half + 1) * WINDOW, :]
            outs = [_attn_group(qh, g, k_all, v_all, mask, sink) for g in range(KV_HEADS)]
            _store_heads(att_s, half * WINDOW, WINDOW, outs)

    u = _gelu(proj(U0, SGU_W))
    vn = _layer_norm(_gelu(proj(V0, SGU_W)), sg_ref[...], sb_ref[...]).astype(BF16)
    mixed = []
    for c in range(TILE // CHUNK):
        row = []
        for g in range(SGU_GROUPS):
            row.append(_dot(ws_ref[g], vn[c * CHUNK:(c + 1) * CHUNK, g * CHUNK:(g + 1) * CHUNK]))
        mixed.append(jnp.concatenate(row, axis=1) + bs_ref[...])
    a_br = u * jnp.concatenate(mixed, axis=0)

    r_br = (hf_ref[...] + hb_ref[...]) * _gelu(proj(Z0, LRU_W))

    y = _sigmoid(proj(G0, D_MODEL)) * _dot(a_br.astype(BF16), wa_ref[...])
    y = y + _sigmoid(proj(G0 + D_MODEL, D_MODEL)) * _dot(att_s[...].astype(BF16), wb_ref[...])
    y = y + _sigmoid(proj(G0 + 2 * D_MODEL, D_MODEL)) * _dot(r_br.astype(BF16), wc_ref[...])
    out = _dot(y.astype(BF16), wo_ref[...]) + bo_ref[...]
    x1 = _layer_norm(ALPHA * x + m[2:3, :] * out, l1g_ref[...], l1b_ref[...])
    x1_ref[...] = x1
    h2 = x1 * (1.0 + m[4:5, :]) + m[3:4, :]
    h2_ref[...] = h2

    logits = _dot3(h2, wrt_ref[...]) + brt_ref[...]
    lane = lax.broadcasted_iota(jnp.int32, (TILE, LANES), 1)
    lane_f = lane.astype(F32)
    big = float(LANES)
    gl = jnp.where(lane < N_GROUPS, logits, NEG_INF)
    gmax = jnp.max(gl, axis=-1, keepdims=True)
    g_idx = jnp.min(jnp.where(gl == gmax, lane_f, big), axis=-1, keepdims=True)
    g_w = 1.0 / jnp.sum(jnp.exp(gl - gmax), axis=-1, keepdims=True)
    lo = N_GROUPS + EXPERTS_PER_GROUP * g_idx
    el = jnp.where((lane_f >= lo) & (lane_f < lo + EXPERTS_PER_GROUP), logits, NEG_INF)
    v1 = jnp.max(el, axis=-1, keepdims=True)
    l1 = jnp.min(jnp.where(el == v1, lane_f, big), axis=-1, keepdims=True)
    el2 = jnp.where(lane_f == l1, NEG_INF, el)
    v2 = jnp.max(el2, axis=-1, keepdims=True)
    l2 = jnp.min(jnp.where(el2 == v2, lane_f, big), axis=-1, keepdims=True)
    e21 = jnp.exp(v2 - v1)
    gate1 = g_w / (1.0 + e21)
    gate2 = g_w * e21 / (1.0 + e21)
    e1 = l1 - N_GROUPS
    e2 = l2 - N_GROUPS

    @pl.when((b == 0) & (i == 0))
    def _():
        run_s[...] = jnp.zeros_like(run_s)

    oh1 = jnp.where(lane_f == e1, 1.0, 0.0)
    oh2 = jnp.where(lane_f == e2, 1.0, 0.0)
    both = oh1 + oh2
    tr = lax.broadcasted_iota(jnp.int32, (TILE, TILE), 0)
    tc = lax.broadcasted_iota(jnp.int32, (TILE, TILE), 1)
    earlier = jnp.where(tc < tr, 1.0, 0.0).astype(BF16)
    before = _dot(earlier, both.astype(BF16)) + run_s[...]
    rank1 = jnp.sum(before * oh1, axis=-1, keepdims=True)
    rank2 = jnp.sum(before * oh2, axis=-1, keepdims=True)
    run_s[...] = run_s[...] + jnp.sum(both, axis=0, keepdims=True)
    cnt_ref[...] = jnp.broadcast_to(run_s[...], cnt_ref.shape)

    ri = jnp.where(lane == 0, e1, jnp.where(lane == 1, e2, jnp.where(lane == 2, rank1,
                                                                      jnp.where(lane == 3, rank2, 0.0))))
    ri_ref[...] = ri.astype(jnp.int32)
    rf_ref[...] = jnp.where(lane == 0, gate1, jnp.where(lane == 1, gate2, 0.0))


def _mixer(lyr, xc, mods, w_main, b_main, cos_t, sin_t, sink, k, v, hf, hb,
           sg, sb, w_sp, b_sp, wa, wb, wc, wo, bo, l1g, l1b, w_rt, b_rt):
    bsz, n_tok, _ = xc.shape
    nt = n_tok // TILE
    n_half = n_tok // WINDOW
    once = pl.Buffered(1)
    tile = lambda b, i, ly: (b, i, 0)
    lay3 = lambda b, i, ly: (ly[0], 0, 0)
    flat = lambda b, i, ly: (b * nt + i, 0)
    kv_ctx = pl.BlockSpec((None, CTX_LEN, KV_W), lambda b, i, ly: (b, 0, 0))
    kv_prev = pl.BlockSpec((None, WINDOW, KV_W), lambda b, i, ly: (b, jnp.maximum(2 * i - 1, 0), 0))
    kv_this = pl.BlockSpec((None, TILE, KV_W), tile)
    kv_next = pl.BlockSpec((None, WINDOW, KV_W), lambda b, i, ly: (b, jnp.minimum(2 * i + 2, n_half - 1), 0))
    row_vec = lambda width: pl.BlockSpec((None, 1, width), lay3)
    return pl.pallas_call(
        functools.partial(_mixer_kernel, n_tiles=nt),
        out_shape=(jax.ShapeDtypeStruct((bsz, n_tok, D_MODEL), F32),
                   jax.ShapeDtypeStruct((bsz * n_tok, D_MODEL), F32),
                   jax.ShapeDtypeStruct((bsz * n_tok, LANES), jnp.int32),
                   jax.ShapeDtypeStruct((bsz * n_tok, LANES), F32),
                   jax.ShapeDtypeStruct((SUBLANES, LANES), F32)),
        grid_spec=pltpu.PrefetchScalarGridSpec(
            num_scalar_prefetch=1,
            grid=(bsz, nt),
            in_specs=[
                pl.BlockSpec((None, TILE, D_MODEL), tile),
                pl.BlockSpec((None, None, N_MOD, D_MODEL),
                             lambda b, i, ly: (ly[0], jnp.where(i == 0, bsz, b), 0, 0)),
                pl.BlockSpec((None, D_MODEL, MAIN_W), lay3, pipeline_mode=once),
                row_vec(MAIN_W),
                pl.BlockSpec((TILE, KV_W), lambda b, i, ly: (i, 0)),
                pl.BlockSpec((TILE, KV_W), lambda b, i, ly: (i, 0)),
                pl.BlockSpec((None, 1, ATT_HEADS), lay3),
                kv_ctx, kv_ctx, kv_prev, kv_prev, kv_this, kv_this, kv_next, kv_next,
                pl.BlockSpec((None, TILE, LRU_W), tile),
                pl.BlockSpec((None, TILE, LRU_W), tile),
                row_vec(SGU_W), row_vec(SGU_W),
                pl.BlockSpec((None, SGU_GROUPS, CHUNK, CHUNK), lambda b, i, ly: (ly[0], 0, 0, 0)),
                pl.BlockSpec((None, CHUNK, SGU_W), lay3),
                pl.BlockSpec((None, SGU_W, D_MODEL), lay3, pipeline_mode=once),
                pl.BlockSpec((None, ATT_W, D_MODEL), lay3, pipeline_mode=once),
                pl.BlockSpec((None, LRU_W, D_MODEL), lay3, pipeline_mode=once),
                pl.BlockSpec((None, D_MODEL, D_MODEL), lay3, pipeline_mode=once),
                row_vec(D_MODEL), row_vec(D_MODEL), row_vec(D_MODEL),
                pl.BlockSpec((None, D_MODEL, LANES), lay3),
                row_vec(LANES),
            ],
            out_specs=(pl.BlockSpec((None, TILE, D_MODEL), tile),
                       pl.BlockSpec((TILE, D_MODEL), flat),
                       pl.BlockSpec((TILE, LANES), flat),
                       pl.BlockSpec((TILE, LANES), flat),
                       pl.BlockSpec((SUBLANES, LANES), lambda b, i, ly: (0, 0))),
            scratch_shapes=[pltpu.VMEM((TILE, ATT_W), F32), pltpu.VMEM((1, LANES), F32)],
        ),
        compiler_params=pltpu.CompilerParams(dimension_semantics=("arbitrary", "arbitrary"),
                                             vmem_limit_bytes=VMEM_LIMIT),
        name="mixer",
    )(lyr, xc, mods, w_main, b_main, cos_t, sin_t, sink, k, v, k, v, k, v, k, v, hf, hb,
      sg, sb, w_sp, b_sp, wa, wb, wc, wo, bo, l1g, l1b, w_rt, b_rt)


def _dispatch_kernel(slot_ref, h2_ref, buf_in_ref, buf_ref, sem):
    del buf_in_ref
    base = pl.program_id(0) * TILE

    def row_copy(t, k):
        return pltpu.make_async_copy(h2_ref.at[pl.ds(t, 1)],
                                     buf_ref.at[pl.ds(slot_ref[TOP_K * (base + t) + k], 1)], sem)

    def start(t, carry):
        for k in range(TOP_K):
            row_copy(t, k).start(priority=k)
        return carry

    def wait(t, carry):
        for k in range(TOP_K):
            row_copy(t, k).wait()
        return carry

    lax.fori_loop(0, TILE, start, 0, unroll=8)
    lax.fori_loop(0, TILE, wait, 0, unroll=8)


def _dispatch(slot, h2, n_slots):
    n_tok = h2.shape[0]
    return pl.pallas_call(
        _dispatch_kernel,
        out_shape=jax.ShapeDtypeStruct((n_slots, D_MODEL), F32),
        grid_spec=pltpu.PrefetchScalarGridSpec(
            num_scalar_prefetch=1,
            grid=(n_tok // TILE,),
            in_specs=[pl.BlockSpec((TILE, D_MODEL), lambda i, sl: (i, 0)),
                      pl.BlockSpec(memory_space=pl.ANY)],
            out_specs=pl.BlockSpec(memory_space=pl.ANY),
            scratch_shapes=[pltpu.SemaphoreType.DMA(())],
        ),
        input_output_aliases={2: 0},
        compiler_params=pltpu.CompilerParams(dimension_semantics=("arbitrary",)),
        name="dispatch",
    )(slot, h2, jnp.zeros((n_slots, D_MODEL), F32))


def _experts_kernel(ly_ref, be_ref, nu_ref, x_ref, w1_ref, w3_ref, w2_ref, o_ref, w1s, w3s, w2s):
    del ly_ref
    i = pl.program_id(0)
    changed = (i == 0) | (be_ref[i] != be_ref[jnp.maximum(i - 1, 0)])

    @pl.when((i < nu_ref[0]) & changed)
    def _():
        w1s[...] = w1_ref[...].astype(BF16)
        w3s[...] = w3_ref[...].astype(BF16)
        w2s[...] = w2_ref[...].astype(BF16)

    @pl.when(i < nu_ref[0])
    def _():
        x = x_ref[...].astype(BF16)
        h1 = _dot(x, w1s[...])
        h3 = _dot(x, w3s[...])
        act = (h1 * _sigmoid(h1) * h3).astype(BF16)
        o_ref[...] = _dot(act, w2s[...])

    @pl.when(i >= nu_ref[0])
    def _():
        o_ref[...] = jnp.zeros_like(o_ref)


def _experts(lyr, block_expert, n_used, buf, w1, w3, w2):
    n_blocks = buf.shape[0] // EXPERT_BLOCK

    def blk(i, ly, be, nu):
        return (jnp.minimum(i, nu[0] - 1), 0)

    def wsel(i, ly, be, nu):
        return (ly[0], be[jnp.minimum(i, nu[0] - 1)], 0, 0)

    return pl.pallas_call(
        _experts_kernel,
        out_shape=jax.ShapeDtypeStruct(buf.shape, F32),
        grid_spec=pltpu.PrefetchScalarGridSpec(
            num_scalar_prefetch=3,
            grid=(n_blocks,),
            in_specs=[pl.BlockSpec((EXPERT_BLOCK, D_MODEL), blk),
                      pl.BlockSpec((None, None, D_MODEL, D_EXPERT), wsel),
                      pl.BlockSpec((None, None, D_MODEL, D_EXPERT), wsel),
                      pl.BlockSpec((None, None, D_EXPERT, D_MODEL), wsel)],
            out_specs=pl.BlockSpec((EXPERT_BLOCK, D_MODEL), lambda i, ly, be, nu: (i, 0)),
            scratch_shapes=[pltpu.VMEM((D_MODEL, D_EXPERT), BF16),
                            pltpu.VMEM((D_MODEL, D_EXPERT), BF16),
                            pltpu.VMEM((D_EXPERT, D_MODEL), BF16)],
        ),
        compiler_params=pltpu.CompilerParams(dimension_semantics=("arbitrary",),
                                             vmem_limit_bytes=40 * MIB),
        name="experts",
    )(lyr, block_expert, n_used, buf, w1, w3, w2)


def _combine_kernel(ly_ref, slot_ref, x1_ref, mod_ref, rf_ref, g_ref, b_ref, eo_ref, o_ref, gbuf, sems,
                    *, n_steps):
    del ly_ref
    s = pl.program_id(0)

    def row_copy(step, t, k):
        par = step % 2
        src = eo_ref.at[pl.ds(slot_ref[TOP_K * (step * TILE + t) + k], 1)]
        return pltpu.make_async_copy(src, gbuf.at[par, k, pl.ds(t, 1)], sems.at[par])

    def start_all(step):
        def body(t, carry):
            for k in range(TOP_K):
                row_copy(step, t, k).start(priority=k)
            return carry
        lax.fori_loop(0, TILE, body, 0, unroll=8)

    @pl.when(s == 0)
    def _():
        start_all(s)

    @pl.when(s + 1 < n_steps)
    def _():
        start_all(s + 1)

    def wait_body(t, carry):
        for k in range(TOP_K):
            row_copy(s, t, k).wait()
        return carry
    lax.fori_loop(0, TILE, wait_body, 0, unroll=8)

    par = s % 2
    rf = rf_ref[...]
    y = rf[:, 0:1] * gbuf[par, 0] + rf[:, 1:2] * gbuf[par, 1]
    m = mod_ref[...]
    o_ref[...] = _layer_norm(ALPHA * x1_ref[...] + m[5:6, :] * y, g_ref[...], b_ref[...])


def _combine(lyr, slot, x1, mods, rf, l2g, l2b, eo):
    bsz, n_tok, _ = x1.shape
    nt = n_tok // TILE
    n_steps = bsz * nt
    tile = lambda s, ly, sl: (s // nt, s % nt, 0)
    lay3 = lambda s, ly, sl: (ly[0], 0, 0)
    return pl.pallas_call(
        functools.partial(_combine_kernel, n_steps=n_steps),
        out_shape=jax.ShapeDtypeStruct((bsz, n_tok, D_MODEL), F32),
        grid_spec=pltpu.PrefetchScalarGridSpec(
            num_scalar_prefetch=2,
            grid=(n_steps,),
            in_specs=[pl.BlockSpec((None, TILE, D_MODEL), tile),
                      pl.BlockSpec((None, None, N_MOD, D_MODEL),
                                   lambda s, ly, sl: (ly[0], jnp.where(s % nt == 0, bsz, s // nt), 0, 0)),
                      pl.BlockSpec((TILE, LANES), lambda s, ly, sl: (s, 0)),
                      pl.BlockSpec((None, 1, D_MODEL), lay3),
                      pl.BlockSpec((None, 1, D_MODEL), lay3),
                      pl.BlockSpec(memory_space=pl.ANY)],
            out_specs=pl.BlockSpec((None, TILE, D_MODEL), tile),
            scratch_shapes=[pltpu.VMEM((2, TOP_K, TILE, D_MODEL), F32),
                            pltpu.SemaphoreType.DMA((2,))],
        ),
        compiler_params=pltpu.CompilerParams(dimension_semantics=("arbitrary",),
                                             vmem_limit_bytes=40 * MIB),
        name="combine",
    )(lyr, slot, x1, mods, rf, l2g, l2b, eo)


def _slot_tables(route_i, counts, n_blocks):
    counts = counts.astype(jnp.int32)
    padded = (counts + EXPERT_BLOCK - 1) // EXPERT_BLOCK * EXPERT_BLOCK
    pad_ends = jnp.cumsum(padded)
    pad_starts = pad_ends - padded
    expert = route_i[:, :TOP_K]
    rank = route_i[:, TOP_K:2 * TOP_K]
    slot = (pad_starts[expert] + rank).reshape(-1)
    block_row = jnp.arange(n_blocks, dtype=jnp.int32) * EXPERT_BLOCK
    block_expert = jnp.minimum(jnp.sum((pad_ends[None, :] <= block_row[:, None]).astype(jnp.int32), axis=1),
                               N_EXPERTS - 1)
    n_used = (pad_ends[-1:] // EXPERT_BLOCK).astype(jnp.int32)
    return slot, block_expert, n_used


def _rope_tables(n_lat):
    rows = n_lat // GRID_W
    row = jnp.repeat(jnp.arange(rows, dtype=F32), GRID_W)
    col = jnp.tile(jnp.arange(GRID_W, dtype=F32), rows)
    axis_dim = HEAD_DIM // 2
    inv_freq = ROPE_THETA ** (-jnp.arange(0, axis_dim, 2, dtype=F32) / axis_dim)
    ang_r = row[:, None] * inv_freq
    ang_c = col[:, None] * inv_freq
    cos = jnp.concatenate([jnp.cos(ang_r)] * 2 + [jnp.cos(ang_c)] * 2, axis=1)
    sin = jnp.concatenate([-jnp.sin(ang_r), jnp.sin(ang_r), -jnp.sin(ang_c), jnp.sin(ang_c)], axis=1)
    cos = jnp.concatenate([jnp.ones((CTX_LEN, HEAD_DIM), F32), cos], axis=0)
    sin = jnp.concatenate([jnp.zeros((CTX_LEN, HEAD_DIM), F32), sin], axis=0)
    return jnp.tile(cos, (1, KV_HEADS)), jnp.tile(sin, (1, KV_HEADS))


def kernel(x, c, ctx, c_ctx, w_mod, b_mod, w_in, b_in, sgu_ln_g, sgu_ln_b, w_spatial, b_spatial, attn_sink, conv_w, conv_b, w_rgate, b_rgate, w_igate, b_igate, lru_lambda, w_proj_a, w_proj_b, w_proj_c, w_out, b_out, ln1_g, ln1_b, ln2_g, ln2_b, w_group, b_group, w_router, b_router, w1, w3, w2):
    bsz, n_lat, _ = x.shape
    n_layers = w_mod.shape[0]
    assert ctx.shape[1] == CTX_LEN == TILE and n_lat % TILE == 0 and bsz < SUBLANES
    n_tok = CTX_LEN + n_lat
    n_blocks = -(-bsz * n_tok * TOP_K // EXPERT_BLOCK) + N_EXPERTS
    n_slots = n_blocks * EXPERT_BLOCK

    cond = jnp.zeros((SUBLANES, D_MODEL), F32).at[:bsz].set(c).at[bsz].set(c_ctx)
    mods = _modulation(cond, w_mod, b_mod).reshape(n_layers, SUBLANES, N_MOD, D_MODEL)
    w_main = w_in[:, :, :MAIN_W].astype(BF16)
    w_ctx = w_in[:, :, MAIN_W:].astype(BF16)
    b_main = b_in[:, None, :MAIN_W]
    b_ctx = b_in[:, None, MAIN_W:]
    w_ri = jnp.concatenate([w_rgate, w_igate], axis=-1).astype(BF16)
    w_sp = w_spatial.astype(BF16)
    b_sp = jnp.repeat(jnp.swapaxes(b_spatial, 1, 2), CHUNK, axis=2)
    wa, wb, wc, wo = (w.astype(BF16) for w in (w_proj_a, w_proj_b, w_proj_c, w_out))
    w_rt = jnp.zeros((n_layers, D_MODEL, LANES), F32)
    w_rt = w_rt.at[:, :, :N_GROUPS].set(w_group).at[:, :, N_GROUPS:N_GROUPS + N_EXPERTS].set(w_router)
    b_rt = jnp.zeros((n_layers, 1, LANES), F32)
    b_rt = b_rt.at[:, 0, :N_GROUPS].set(b_group).at[:, 0, N_GROUPS:N_GROUPS + N_EXPERTS].set(b_router)
    cos_t, sin_t = _rope_tables(n_lat)
    row3 = lambda a: a[:, None, :]
    conv_b4, b_r4, b_i4 = (a[:, :, None, :] for a in (conv_b, b_rgate, b_igate))
    sink, sg, sb, bo = row3(attn_sink), row3(sgu_ln_g), row3(sgu_ln_b), row3(b_out)
    l1g, l1b, l2g, l2b = row3(ln1_g), row3(ln1_b), row3(ln2_g), row3(ln2_b)

    def layer(l, xc):
        lyr = jnp.full((1,), l, jnp.int32)
        k, v, hf, hb = _kv_lru(lyr, xc, mods, w_ctx, b_ctx, cos_t, sin_t, conv_w, conv_b4,
                               w_ri, b_r4, b_i4, lru_lambda)
        x1, h2, route_i, route_f, counts = _mixer(
            lyr, xc, mods, w_main, b_main, cos_t, sin_t, sink, k, v, hf, hb,
            sg, sb, w_sp, b_sp, wa, wb, wc, wo, bo, l1g, l1b, w_rt, b_rt)
        slot, block_expert, n_used = _slot_tables(route_i, counts[0, :N_EXPERTS], n_blocks)
        buf = _dispatch(slot, h2, n_slots)
        eo = _experts(lyr, block_expert, n_used, buf, w1, w3, w2)
        return _combine(lyr, slot, x1, mods, route_f, l2g, l2b, eo)

    xc = jnp.concatenate([ctx, x], axis=1)
    for l in range(n_layers):
        xc = layer(l, xc)
    return xc[:, CTX_LEN:]
```

```python
import functools

import jax
import jax.numpy as jnp
from jax import lax
from jax.experimental import pallas as pl
from jax.experimental.pallas import tpu as pltpu

F32 = jnp.float32
BF16 = jnp.bfloat16

D_MODEL = 1024
DEPTH = 4
GRID_W = 64
CTX_LEN = 256
HEAD_DIM = 64
ATT_HEADS = 8
KV_HEADS = 2
Q_PER_KV = ATT_HEADS // KV_HEADS
ATT_W = ATT_HEADS * HEAD_DIM
KV_W = KV_HEADS * HEAD_DIM
WINDOW = 128
ROPE_THETA = 10000.0
CHUNK = 128
SGU_GROUPS = 4
SGU_W = SGU_GROUPS * CHUNK
LRU_W = D_MODEL
LRU_BLOCKS = 8
LRU_BW = LRU_W // LRU_BLOCKS
CONV_W = 4
LRU_C = 8.0
N_GROUPS = 4
EXPERTS_PER_GROUP = 8
N_EXPERTS = N_GROUPS * EXPERTS_PER_GROUP
TOP_K = 2
D_EXPERT = 512
N_MOD = 6
ALPHA = (2.0 * DEPTH) ** 0.25
LN_EPS = 1e-6
NEG_INF = -1e30

Q0, U0, V0, G0, Z0 = 0, ATT_W, ATT_W + SGU_W, ATT_W + 2 * SGU_W, ATT_W + 2 * SGU_W + 3 * D_MODEL
MAIN_W = Z0 + LRU_W
CTX_W = 2 * KV_W + LRU_W

TILE = 256
SUBLANES = 8
LANES = 128
EXPERT_BLOCK = 256
GATHER_ROWS = 64
MIB = 1024 * 1024
VMEM_LIMIT = 56 * MIB


def _sigmoid(x):
    return 0.5 * (1.0 + jnp.tanh(0.5 * x))


def _gelu(x):
    return 0.5 * x * (1.0 + jnp.tanh(0.7978845608028654 * (x + 0.044715 * (x * x * x))))


def _layer_norm(x, g, b):
    mu = jnp.mean(x, axis=-1, keepdims=True)
    xc = x - mu
    var = jnp.mean(xc * xc, axis=-1, keepdims=True)
    return xc * lax.rsqrt(var + LN_EPS) * g + b


def _dot(a, b):
    return jnp.dot(a, b, preferred_element_type=F32)


def _dot3(a, b):
    a_hi = a.astype(BF16)
    a_lo = (a - a_hi.astype(F32)).astype(BF16)
    b_hi = b.astype(BF16)
    b_lo = (b - b_hi.astype(F32)).astype(BF16)
    return _dot(a_hi, b_hi) + (_dot(a_hi, b_lo) + _dot(a_lo, b_hi))


def _pack_rows(v):
    n = v.shape[1] // 2
    bits = pltpu.bitcast(v.astype(BF16).astype(F32), jnp.uint32)
    return (bits[:, :n] >> 16) | bits[:, n:]


def _unpack_rows(p):
    lo = pltpu.bitcast(p << 16, F32)
    hi = pltpu.bitcast(p & jnp.uint32(0xFFFF0000), F32)
    return jnp.concatenate([lo, hi], axis=1)


def _rope(x, cos, sin_signed):
    n = x.shape[1] // LANES
    if n > 1:
        cos = jnp.concatenate([cos] * n, axis=1)
        sin_signed = jnp.concatenate([sin_signed] * n, axis=1)
    lane = lax.broadcasted_iota(jnp.int32, x.shape, 1)
    first = (lane & 31) < 16
    w = x.shape[1]
    partner = jnp.where(first, pltpu.roll(x, w - 16, 1), pltpu.roll(x, 16, 1))
    return x * cos + partner * sin_signed


def _mods_kernel(c_ref, w_ref, b_ref, o_ref):
    c = c_ref[...]
    o_ref[...] = _dot3(c * _sigmoid(c), w_ref[...]) + b_ref[...]


def _modulation(cond, w_mod, b_mod):
    n_layers = w_mod.shape[0]
    tn = 1536
    return pl.pallas_call(
        _mods_kernel,
        out_shape=jax.ShapeDtypeStruct((n_layers, SUBLANES, N_MOD * D_MODEL), F32),
        grid=(n_layers, N_MOD * D_MODEL // tn),
        in_specs=[
            pl.BlockSpec((SUBLANES, D_MODEL), lambda l, j: (0, 0)),
            pl.BlockSpec((None, D_MODEL, tn), lambda l, j: (l, 0, j)),
            pl.BlockSpec((None, 1, tn), lambda l, j: (l, 0, j)),
        ],
        out_specs=pl.BlockSpec((None, SUBLANES, tn), lambda l, j: (l, 0, j)),
        compiler_params=pltpu.CompilerParams(dimension_semantics=("arbitrary", "arbitrary"),
                                             vmem_limit_bytes=40 * MIB),
        name="modulation",
    )(cond, w_mod, b_mod.reshape(n_layers, 1, N_MOD * D_MODEL))


def _scan_tile(a_ref, u_ref, carry_ref, out_ref, reverse):
    width = a_ref.shape[1]
    row = lax.broadcasted_iota(jnp.int32, (SUBLANES, width), 0)
    n_blk = a_ref.shape[0] // SUBLANES

    def body(j, carry):
        blk = (n_blk - 1 - j) if reverse else j
        r0 = pl.multiple_of(blk * SUBLANES, SUBLANES)
        a = a_ref[pl.ds(r0, SUBLANES), :]
        u = u_ref[pl.ds(r0, SUBLANES), :]
        for s in (1, 2, 4):
            if reverse:
                keep = row < SUBLANES - s
                a_sh = jnp.where(keep, pltpu.roll(a, SUBLANES - s, 0), 1.0)
                u_sh = jnp.where(keep, pltpu.roll(u, SUBLANES - s, 0), 0.0)
            else:
                keep = row >= s
                a_sh = jnp.where(keep, pltpu.roll(a, s, 0), 1.0)
                u_sh = jnp.where(keep, pltpu.roll(u, s, 0), 0.0)
            u = a * u_sh + u
            a = a * a_sh
        h = u + a * carry
        out_ref[pl.ds(r0, SUBLANES), :] = h
        return h[0:1, :] if reverse else h[SUBLANES - 1:SUBLANES, :]

    carry_ref[...] = lax.fori_loop(0, n_blk, body, carry_ref[...])


def _lru_inputs(t, wri_ref, br_ref, bi_ref, sp, a_ref, u_ref):
    for n in range(LRU_BLOCKS):
        cols = slice(n * LRU_BW, (n + 1) * LRU_BW)
        tn = t[:, cols]
        g = _dot(tn.astype(BF16), wri_ref[n])
        r = _sigmoid(g[:, :LRU_BW] + br_ref[:, cols])
        i = _sigmoid(g[:, LRU_BW:] + bi_ref[:, cols])
        a = jnp.exp((-LRU_C) * r * sp[:, cols])
        a_ref[:, cols] = a
        u_ref[:, cols] = jnp.sqrt(1.0 - a * a) * i * tn


def _kv_lru_kernel(ly_ref, xf_ref, xb_ref, mod_ref, wc_ref, bc_ref, cos_ref, sin_ref,
                   cw_ref, cb_ref, wri_ref, br_ref, bi_ref, lam_ref,
                   k_ref, v_ref, hf_ref, hb_ref,
                   xsf, xsb, af, uf, ab, ub, cf, cbk):
    del ly_ref
    i = pl.program_id(1)
    m = mod_ref[...]
    shift, scale = m[0:1, :], 1.0 + m[1:2, :]

    @pl.when(i == 0)
    def _():
        cf[...] = jnp.zeros_like(cf)
        cbk[...] = jnp.zeros_like(cbk)

    h = (xf_ref[...] * scale + shift).astype(BF16)
    p = _dot(h, wc_ref[...]) + bc_ref[...]
    k_ref[...] = _rope(p[:, :KV_W], cos_ref[...], sin_ref[...]).astype(BF16)
    v_ref[...] = p[:, KV_W:2 * KV_W].astype(BF16)
    xr_f = p[:, 2 * KV_W:]
    h = (xb_ref[...] * scale + shift).astype(BF16)
    xr_b = _dot(h, wc_ref[:, 2 * KV_W:]) + bc_ref[:, 2 * KV_W:]

    @pl.when(i <= 1)
    def _():
        xsf[0:SUBLANES, :] = jnp.zeros((SUBLANES, LRU_W), F32)
        xsb[TILE:TILE + SUBLANES, :] = jnp.zeros((SUBLANES, LRU_W), F32)

    @pl.when(i > 1)
    def _():
        xsf[0:SUBLANES, :] = xsf[TILE:TILE + SUBLANES, :]
        xsb[TILE:TILE + SUBLANES, :] = xsb[0:SUBLANES, :]

    xsf[SUBLANES:SUBLANES + TILE, :] = xr_f
    xsb[0:TILE, :] = xr_b

    lam = lam_ref[...]
    sp = jnp.maximum(-lam, 0.0) + jnp.log1p(jnp.exp(-jnp.abs(lam)))

    cw = cw_ref[0]
    t = (cb_ref[0] + cw[0:1, :] * xsf[SUBLANES - 3:SUBLANES - 3 + TILE, :]
         + cw[1:2, :] * xsf[SUBLANES - 2:SUBLANES - 2 + TILE, :]
         + cw[2:3, :] * xsf[SUBLANES - 1:SUBLANES - 1 + TILE, :]
         + cw[3:4, :] * xr_f)
    _lru_inputs(t, wri_ref.at[0], br_ref.at[0], bi_ref.at[0], sp[0:1, :], af, uf)
    cw = cw_ref[1]
    t = (cb_ref[1] + cw[0:1, :] * xr_b
         + cw[1:2, :] * xsb[1:1 + TILE, :]
         + cw[2:3, :] * xsb[2:2 + TILE, :]
         + cw[3:4, :] * xsb[3:3 + TILE, :])
    _lru_inputs(t, wri_ref.at[1], br_ref.at[1], bi_ref.at[1], sp[1:2, :], ab, ub)

    _scan_tile(af, uf, cf, hf_ref, reverse=False)
    _scan_tile(ab, ub, cbk, hb_ref, reverse=True)


def _kv_lru(lyr, xc, mods, w_ctx, b_ctx, cos_t, sin_t, conv_w, conv_b, w_ri, b_r, b_i, lam):
    bsz, n_tok, _ = xc.shape
    nt = n_tok // TILE
    tile_f = lambda b, i, ly: (b, i, 0)
    tile_b = lambda b, i, ly: (b, jnp.where(i == 0, 0, nt - i), 0)
    lay3 = lambda b, i, ly: (ly[0], 0, 0)
    lay4 = lambda b, i, ly: (ly[0], 0, 0, 0)
    return pl.pallas_call(
        _kv_lru_kernel,
        out_shape=(jax.ShapeDtypeStruct((bsz, n_tok, KV_W), BF16),
                   jax.ShapeDtypeStruct((bsz, n_tok, KV_W), BF16),
                   jax.ShapeDtypeStruct((bsz, n_tok, LRU_W), F32),
                   jax.ShapeDtypeStruct((bsz, n_tok, LRU_W), F32)),
        grid_spec=pltpu.PrefetchScalarGridSpec(
            num_scalar_prefetch=1,
            grid=(bsz, nt),
            in_specs=[
                pl.BlockSpec((None, TILE, D_MODEL), tile_f),
                pl.BlockSpec((None, TILE, D_MODEL), tile_b),
                pl.BlockSpec((None, None, N_MOD, D_MODEL),
                             lambda b, i, ly: (ly[0], jnp.where(i == 0, bsz, b), 0, 0)),
                pl.BlockSpec((None, D_MODEL, CTX_W), lay3),
                pl.BlockSpec((None, 1, CTX_W), lay3),
                pl.BlockSpec((TILE, KV_W), lambda b, i, ly: (i, 0)),
                pl.BlockSpec((TILE, KV_W), lambda b, i, ly: (i, 0)),
                pl.BlockSpec((None, 2, CONV_W, LRU_W), lay4),
                pl.BlockSpec((None, 2, 1, LRU_W), lay4),
                pl.BlockSpec((None, 2, LRU_BLOCKS, LRU_BW, 2 * LRU_BW), lambda b, i, ly: (ly[0], 0, 0, 0, 0)),
                pl.BlockSpec((None, 2, 1, LRU_W), lay4),
                pl.BlockSpec((None, 2, 1, LRU_W), lay4),
                pl.BlockSpec((None, 2, LRU_W), lay3),
            ],
            out_specs=(pl.BlockSpec((None, TILE, KV_W), tile_f),
                       pl.BlockSpec((None, TILE, KV_W), tile_f),
                       pl.BlockSpec((None, TILE, LRU_W), tile_f),
                       pl.BlockSpec((None, TILE, LRU_W), tile_b)),
            scratch_shapes=[pltpu.VMEM((TILE + SUBLANES, LRU_W), F32),
                            pltpu.VMEM((TILE + SUBLANES, LRU_W), F32),
                            pltpu.VMEM((TILE, LRU_W), F32), pltpu.VMEM((TILE, LRU_W), F32),
                            pltpu.VMEM((TILE, LRU_W), F32), pltpu.VMEM((TILE, LRU_W), F32),
                            pltpu.VMEM((1, LRU_W), F32), pltpu.VMEM((1, LRU_W), F32)],
        ),
        compiler_params=pltpu.CompilerParams(dimension_semantics=("arbitrary", "arbitrary"),
                                             vmem_limit_bytes=VMEM_LIMIT),
        name="kv_lru",
    )(lyr, xc, xc, mods, w_ctx, b_ctx, cos_t, sin_t, conv_w, conv_b, w_ri, b_r, b_i, lam)


def _attn_group(q, g, k_all, v_all, bias, sink):
    rows = q.shape[0]
    lane = lax.broadcasted_iota(jnp.int32, (rows, LANES), 1)
    mine = (lane >= g * HEAD_DIM) & (lane < (g + 1) * HEAD_DIM)
    parts = []
    for j in range(Q_PER_KV):
        hd = g * Q_PER_KV + j
        t = q[:, (hd // 2) * LANES:(hd // 2 + 1) * LANES]
        if hd % 2 != g:
            t = pltpu.roll(t, HEAD_DIM, 1)
        parts.append(jnp.where(mine, t, 0.0))
    qs = jnp.concatenate(parts, axis=0).astype(BF16)
    s = lax.dot_general(qs, k_all, (((1,), (1,)), ((), ())), preferred_element_type=F32)
    if bias is not None:
        s = s + bias
    blk = lax.broadcasted_iota(jnp.int32, (Q_PER_KV * rows, 1), 0)
    sink_col = jnp.zeros((Q_PER_KV * rows, 1), F32)
    for j in range(Q_PER_KV):
        hd = g * Q_PER_KV + j
        sink_col = jnp.where((blk >= j * rows) & (blk < (j + 1) * rows), sink[:, hd:hd + 1], sink_col)
    mx = jnp.maximum(jnp.max(s, axis=-1, keepdims=True), sink_col)
    p = jnp.exp(s - mx)
    den = jnp.sum(p, axis=-1, keepdims=True) + jnp.exp(sink_col - mx)
    o = _dot(p.astype(BF16), v_all)
    return o * (1.0 / den)


def _store_heads(att_s, r0, rows, outs):
    lane = lax.broadcasted_iota(jnp.int32, (rows, LANES), 1)
    low = lane < HEAD_DIM
    for t in range(ATT_HEADS // 2):
        g = (2 * t) // Q_PER_KV
        j = 2 * t - g * Q_PER_KV
        even = outs[g][j * rows:(j + 1) * rows, :]
        odd = outs[g][(j + 1) * rows:(j + 2) * rows, :]
        if g == 1:
            even = pltpu.roll(even, HEAD_DIM, 1)
        else:
            odd = pltpu.roll(odd, HEAD_DIM, 1)
        att_s[r0:r0 + rows, t * LANES:(t + 1) * LANES] = jnp.where(low, even, odd)


def _mixer_kernel(ly_ref, x_ref, mod_ref, w_ref, b_ref, cos_ref, sin_ref, sink_ref, bias0_ref, bias1_ref,
                  kc_ref, vc_ref, kp_ref, vp_ref, kt_ref, vt_ref, kn_ref, vn_ref,
                  hf_ref, hb_ref, sg_ref, sb_ref, ws_ref, bs_ref,
                  wa_ref, wb_ref, wc_ref, wo_ref, bo_ref, l1g_ref, l1b_ref, wrt_ref, brt_ref,
                  x1_ref, h2_ref, ri_ref, rf_ref, cnt_ref,
                  att_s, run_s):
    del ly_ref
    b = pl.program_id(0)
    i = pl.program_id(1)
    x = x_ref[...]
    m = mod_ref[...]
    h = (x * (1.0 + m[1:2, :]) + m[0:1, :]).astype(BF16)

    def proj(c0, width):
        return _dot(h, w_ref[:, c0:c0 + width]) + b_ref[:, c0:c0 + width]

    q = _rope(proj(Q0, ATT_W), cos_ref[...], sin_ref[...]) * (HEAD_DIM ** -0.5)
    sink = sink_ref[...]

    @pl.when(i == 0)
    def _():
        outs = [_attn_group(q, g, kc_ref[...], vc_ref[...], None, sink) for g in range(KV_HEADS)]
        _store_heads(att_s, 0, TILE, outs)

    @pl.when(i > 0)
    def _():
        kt, vt = kt_ref[...], vt_ref[...]
        for half in range(2):
            if half == 0:
                k_all = jnp.concatenate([kc_ref[...], kp_ref[...], kt], axis=0)
                v_all = jnp.concatenate([vc_ref[...], vp_ref[...], vt], axis=0)
                bias = bias0_ref[...]
            else:
                k_all = jnp.concatenate([kc_ref[...], kt, kn_ref[...]], axis=0)
                v_all = jnp.concatenate([vc_ref[...], vt, vn_ref[...]], axis=0)
                bias = bias1_ref[...]
            qh = q[half * WINDOW:(half + 1) * WINDOW, :]
            outs = [_attn_group(qh, g, k_all, v_all, bias, sink) for g in range(KV_HEADS)]
            _store_heads(att_s, half * WINDOW, WINDOW, outs)

    u = _gelu(proj(U0, SGU_W))
    vn = _layer_norm(_gelu(proj(V0, SGU_W)), sg_ref[...], sb_ref[...]).astype(BF16)
    mixed = []
    for c in range(TILE // CHUNK):
        row = []
        for g in range(SGU_GROUPS):
            row.append(_dot(ws_ref[g], vn[c * CHUNK:(c + 1) * CHUNK, g * CHUNK:(g + 1) * CHUNK]))
        mixed.append(jnp.concatenate(row, axis=1) + bs_ref[...])
    a_br = u * jnp.concatenate(mixed, axis=0)

    r_br = (hf_ref[...] + hb_ref[...]) * _gelu(proj(Z0, LRU_W))

    y = _sigmoid(proj(G0, D_MODEL)) * _dot(a_br.astype(BF16), wa_ref[...])
    y = y + _sigmoid(proj(G0 + D_MODEL, D_MODEL)) * _dot(att_s[...].astype(BF16), wb_ref[...])
    y = y + _sigmoid(proj(G0 + 2 * D_MODEL, D_MODEL)) * _dot(r_br.astype(BF16), wc_ref[...])
    out = _dot(y.astype(BF16), wo_ref[...]) + bo_ref[...]
    x1 = _layer_norm(ALPHA * x + m[2:3, :] * out, l1g_ref[...], l1b_ref[...])
    x1_ref[...] = x1
    h2 = x1 * (1.0 + m[4:5, :]) + m[3:4, :]
    h2_ref[...] = _pack_rows(h2)

    logits = _dot3(h2, wrt_ref[...]) + brt_ref[...]
    lane = lax.broadcasted_iota(jnp.int32, (TILE, LANES), 1)
    lane_f = lane.astype(F32)
    big = float(LANES)
    gl = jnp.where(lane < N_GROUPS, logits, NEG_INF)
    gmax = jnp.max(gl, axis=-1, keepdims=True)
    g_idx = jnp.min(jnp.where(gl == gmax, lane_f, big), axis=-1, keepdims=True)
    g_w = 1.0 / jnp.sum(jnp.exp(gl - gmax), axis=-1, keepdims=True)
    lo = N_GROUPS + EXPERTS_PER_GROUP * g_idx
    el = jnp.where((lane_f >= lo) & (lane_f < lo + EXPERTS_PER_GROUP), logits, NEG_INF)
    v1 = jnp.max(el, axis=-1, keepdims=True)
    l1 = jnp.min(jnp.where(el == v1, lane_f, big), axis=-1, keepdims=True)
    el2 = jnp.where(lane_f == l1, NEG_INF, el)
    v2 = jnp.max(el2, axis=-1, keepdims=True)
    l2 = jnp.min(jnp.where(el2 == v2, lane_f, big), axis=-1, keepdims=True)
    e21 = jnp.exp(v2 - v1)
    gate1 = g_w / (1.0 + e21)
    gate2 = g_w * e21 / (1.0 + e21)
    e1 = l1 - N_GROUPS
    e2 = l2 - N_GROUPS

    @pl.when((b == 0) & (i == 0))
    def _():
        run_s[...] = jnp.zeros_like(run_s)

    oh1 = jnp.where(lane_f == e1, 1.0, 0.0)
    oh2 = jnp.where(lane_f == e2, 1.0, 0.0)
    both = oh1 + oh2
    tr = lax.broadcasted_iota(jnp.int32, (TILE, TILE), 0)
    tc = lax.broadcasted_iota(jnp.int32, (TILE, TILE), 1)
    earlier = jnp.where(tc < tr, 1.0, 0.0).astype(BF16)
    before = _dot(earlier, both.astype(BF16)) + run_s[...]
    rank1 = jnp.sum(before * oh1, axis=-1, keepdims=True)
    rank2 = jnp.sum(before * oh2, axis=-1, keepdims=True)
    run_s[...] = run_s[...] + jnp.sum(both, axis=0, keepdims=True)
    cnt_ref[...] = jnp.broadcast_to(run_s[...], cnt_ref.shape)

    ri = jnp.where(lane == 0, e1, jnp.where(lane == 1, e2, jnp.where(lane == 2, rank1,
                                                                      jnp.where(lane == 3, rank2, 0.0))))
    ri_ref[...] = ri.astype(jnp.int32)
    rf_ref[...] = jnp.where(lane == 0, gate1, jnp.where(lane == 1, gate2, 0.0))


def _window_bias():
    r = jnp.arange(Q_PER_KV * WINDOW)[:, None] % WINDOW
    c = jnp.arange(CTX_LEN + 3 * WINDOW)[None, :]
    j_prev = c - CTX_LEN
    j_next = c - (CTX_LEN + 2 * WINDOW)
    in_prev = (j_prev >= 0) & (j_prev < WINDOW)
    in_next = j_next >= 0
    full = ~(in_prev & (j_prev < r)) & ~(in_next & (j_next > r))
    masks = jnp.stack([jnp.stack([full, full & ~in_prev]), jnp.stack([full, full & ~in_next])])
    return jnp.where(masks, 0.0, NEG_INF).astype(F32)


def _mixer(lyr, xc, mods, w_main, b_main, cos_t, sin_t, sink, k, v, hf, hb,
           sg, sb, w_sp, b_sp, wa, wb, wc, wo, bo, l1g, l1b, w_rt, b_rt):
    bsz, n_tok, _ = xc.shape
    nt = n_tok // TILE
    n_half = n_tok // WINDOW
    once = pl.Buffered(1)
    bias = _window_bias()
    n_q, n_keys = bias.shape[2:]
    bias0 = pl.BlockSpec((None, None, n_q, n_keys), lambda b, i, ly: (0, jnp.where(i == 1, 1, 0), 0, 0))
    bias1 = pl.BlockSpec((None, None, n_q, n_keys), lambda b, i, ly: (1, jnp.where(i == nt - 1, 1, 0), 0, 0))
    tile = lambda b, i, ly: (b, i, 0)
    lay3 = lambda b, i, ly: (ly[0], 0, 0)
    flat = lambda b, i, ly: (b * nt + i, 0)
    kv_ctx = pl.BlockSpec((None, CTX_LEN, KV_W), lambda b, i, ly: (b, 0, 0))
    kv_prev = pl.BlockSpec((None, WINDOW, KV_W), lambda b, i, ly: (b, jnp.maximum(2 * i - 1, 0), 0))
    kv_this = pl.BlockSpec((None, TILE, KV_W), tile)
    kv_next = pl.BlockSpec((None, WINDOW, KV_W), lambda b, i, ly: (b, jnp.minimum(2 * i + 2, n_half - 1), 0))
    row_vec = lambda width: pl.BlockSpec((None, 1, width), lay3)
    return pl.pallas_call(
        _mixer_kernel,
        out_shape=(jax.ShapeDtypeStruct((bsz, n_tok, D_MODEL), F32),
                   jax.ShapeDtypeStruct((bsz * n_tok, D_MODEL // 2), jnp.uint32),
                   jax.ShapeDtypeStruct((bsz * n_tok, LANES), jnp.int32),
                   jax.ShapeDtypeStruct((bsz * n_tok, LANES), F32),
                   jax.ShapeDtypeStruct((SUBLANES, LANES), F32)),
        grid_spec=pltpu.PrefetchScalarGridSpec(
            num_scalar_prefetch=1,
            grid=(bsz, nt),
            in_specs=[
                pl.BlockSpec((None, TILE, D_MODEL), tile),
                pl.BlockSpec((None, None, N_MOD, D_MODEL),
                             lambda b, i, ly: (ly[0], jnp.where(i == 0, bsz, b), 0, 0)),
                pl.BlockSpec((None, D_MODEL, MAIN_W), lay3, pipeline_mode=once),
                row_vec(MAIN_W),
                pl.BlockSpec((TILE, KV_W), lambda b, i, ly: (i, 0)),
                pl.BlockSpec((TILE, KV_W), lambda b, i, ly: (i, 0)),
                pl.BlockSpec((None, 1, ATT_HEADS), lay3),
                bias0, bias1,
                kv_ctx, kv_ctx, kv_prev, kv_prev, kv_this, kv_this, kv_next, kv_next,
                pl.BlockSpec((None, TILE, LRU_W), tile),
                pl.BlockSpec((None, TILE, LRU_W), tile),
                row_vec(SGU_W), row_vec(SGU_W),
                pl.BlockSpec((None, SGU_GROUPS, CHUNK, CHUNK), lambda b, i, ly: (ly[0], 0, 0, 0)),
                pl.BlockSpec((None, CHUNK, SGU_W), lay3),
                pl.BlockSpec((None, SGU_W, D_MODEL), lay3, pipeline_mode=once),
                pl.BlockSpec((None, ATT_W, D_MODEL), lay3, pipeline_mode=once),
                pl.BlockSpec((None, LRU_W, D_MODEL), lay3, pipeline_mode=once),
                pl.BlockSpec((None, D_MODEL, D_MODEL), lay3, pipeline_mode=once),
                row_vec(D_MODEL), row_vec(D_MODEL), row_vec(D_MODEL),
                pl.BlockSpec((None, D_MODEL, LANES), lay3),
                row_vec(LANES),
            ],
            out_specs=(pl.BlockSpec((None, TILE, D_MODEL), tile),
                       pl.BlockSpec((TILE, D_MODEL // 2), flat),
                       pl.BlockSpec((TILE, LANES), flat),
                       pl.BlockSpec((TILE, LANES), flat),
                       pl.BlockSpec((SUBLANES, LANES), lambda b, i, ly: (0, 0))),
            scratch_shapes=[pltpu.VMEM((TILE, ATT_W), F32), pltpu.VMEM((1, LANES), F32)],
        ),
        compiler_params=pltpu.CompilerParams(dimension_semantics=("arbitrary", "arbitrary"),
                                             vmem_limit_bytes=VMEM_LIMIT),
        name="mixer",
    )(lyr, xc, mods, w_main, b_main, cos_t, sin_t, sink, bias, bias, k, v, k, v, k, v, k, v, hf, hb,
      sg, sb, w_sp, b_sp, wa, wb, wc, wo, bo, l1g, l1b, w_rt, b_rt)


def _dispatch_kernel(slot_ref, h2_ref, buf_in_ref, buf_ref, sem):
    del buf_in_ref
    base = pl.program_id(0) * TILE

    def row_copy(t, k):
        return pltpu.make_async_copy(h2_ref.at[pl.ds(t, 1)],
                                     buf_ref.at[pl.ds(slot_ref[TOP_K * (base + t) + k], 1)], sem)

    def start(t, carry):
        for k in range(TOP_K):
            row_copy(t, k).start(priority=k)
        return carry

    def wait(t, carry):
        for k in range(TOP_K):
            row_copy(t, k).wait()
        return carry

    lax.fori_loop(0, TILE, start, 0, unroll=8)
    lax.fori_loop(0, TILE, wait, 0, unroll=8)


def _dispatch(slot, h2, n_slots):
    n_tok, width = h2.shape
    return pl.pallas_call(
        _dispatch_kernel,
        out_shape=jax.ShapeDtypeStruct((n_slots, width), h2.dtype),
        grid_spec=pltpu.PrefetchScalarGridSpec(
            num_scalar_prefetch=1,
            grid=(n_tok // TILE,),
            in_specs=[pl.BlockSpec((TILE, width), lambda i, sl: (i, 0)),
                      pl.BlockSpec(memory_space=pl.ANY)],
            out_specs=pl.BlockSpec(memory_space=pl.ANY),
            scratch_shapes=[pltpu.SemaphoreType.DMA(())],
        ),
        input_output_aliases={2: 0},
        compiler_params=pltpu.CompilerParams(dimension_semantics=("arbitrary",)),
        name="dispatch",
    )(slot, h2, jnp.zeros((n_slots, width), h2.dtype))


def _experts_kernel(ly_ref, be_ref, nu_ref, x_ref, w1_ref, w3_ref, w2_ref, o_ref, w1s, w3s, w2s):
    del ly_ref
    i = pl.program_id(0)
    changed = (i == 0) | (be_ref[i] != be_ref[jnp.maximum(i - 1, 0)])

    @pl.when((i < nu_ref[0]) & changed)
    def _():
        w1s[...] = w1_ref[...].astype(BF16)
        w3s[...] = w3_ref[...].astype(BF16)
        w2s[...] = w2_ref[...].astype(BF16)

    @pl.when(i < nu_ref[0])
    def _():
        x = _unpack_rows(x_ref[...]).astype(BF16)
        h1 = _dot(x, w1s[...])
        h3 = _dot(x, w3s[...])
        act = (h1 * _sigmoid(h1) * h3).astype(BF16)
        o_ref[...] = _pack_rows(_dot(act, w2s[...]))

    @pl.when(i >= nu_ref[0])
    def _():
        o_ref[...] = jnp.zeros_like(o_ref)


def _experts(lyr, block_expert, n_used, buf, w1, w3, w2):
    n_blocks = buf.shape[0] // EXPERT_BLOCK

    def blk(i, ly, be, nu):
        return (jnp.minimum(i, nu[0] - 1), 0)

    def wsel(i, ly, be, nu):
        return (ly[0], be[jnp.minimum(i, nu[0] - 1)], 0, 0)

    return pl.pallas_call(
        _experts_kernel,
        out_shape=jax.ShapeDtypeStruct(buf.shape, buf.dtype),
        grid_spec=pltpu.PrefetchScalarGridSpec(
            num_scalar_prefetch=3,
            grid=(n_blocks,),
            in_specs=[pl.BlockSpec((EXPERT_BLOCK, D_MODEL // 2), blk),
                      pl.BlockSpec((None, None, D_MODEL, D_EXPERT), wsel),
                      pl.BlockSpec((None, None, D_MODEL, D_EXPERT), wsel),
                      pl.BlockSpec((None, None, D_EXPERT, D_MODEL), wsel)],
            out_specs=pl.BlockSpec((EXPERT_BLOCK, D_MODEL // 2), lambda i, ly, be, nu: (i, 0)),
            scratch_shapes=[pltpu.VMEM((D_MODEL, D_EXPERT), BF16),
                            pltpu.VMEM((D_MODEL, D_EXPERT), BF16),
                            pltpu.VMEM((D_EXPERT, D_MODEL), BF16)],
        ),
        compiler_params=pltpu.CompilerParams(dimension_semantics=("arbitrary",),
                                             vmem_limit_bytes=40 * MIB),
        name="experts",
    )(lyr, block_expert, n_used, buf, w1, w3, w2)


def _combine_kernel(ly_ref, slot_ref, x1_ref, mod_ref, rf_ref, g_ref, b_ref, eo_ref, o_ref, gbuf, sems,
                    *, n_steps):
    del ly_ref
    s = pl.program_id(0)
    par = s % 2
    n_groups = TILE // GATHER_ROWS
    gate2 = mod_ref[5:6, :]
    ln_g, ln_b = g_ref[...], b_ref[...]

    def group_copies(tile, buf, j):
        copies = []
        for r in range(GATHER_ROWS):
            t = j * GATHER_ROWS + r
            for k in range(TOP_K):
                src = eo_ref.at[pl.ds(slot_ref[TOP_K * (tile * TILE + t) + k], 1)]
                copies.append((k, pltpu.make_async_copy(src, gbuf.at[buf, k, pl.ds(t, 1)], sems.at[buf, j])))
        return copies

    def start_group(tile, buf, j):
        for k, cp in group_copies(tile, buf, j):
            cp.start(priority=k)

    def wait_group(tile, buf, j):
        for _, cp in group_copies(tile, buf, j):
            cp.wait()

    @pl.when(s == 0)
    def _():
        lax.fori_loop(0, n_groups, lambda j, c: (start_group(s, par, j), c)[1], 0)

    nxt = jnp.minimum(s + 1, n_steps - 1)

    def body(j, carry):
        start_group(nxt, 1 - par, j)
        wait_group(s, par, j)
        rows = pl.ds(pl.multiple_of(j * GATHER_ROWS, GATHER_ROWS), GATHER_ROWS)
        rf = rf_ref[rows, :]
        y = (rf[:, 0:1] * _unpack_rows(gbuf[par, 0, rows, :])
             + rf[:, 1:2] * _unpack_rows(gbuf[par, 1, rows, :]))
        o_ref[rows, :] = _layer_norm(ALPHA * x1_ref[rows, :] + gate2 * y, ln_g, ln_b)
        return carry

    lax.fori_loop(0, n_groups, body, 0)

    @pl.when(s == n_steps - 1)
    def _():
        lax.fori_loop(0, n_groups, lambda j, c: (wait_group(nxt, 1 - par, j), c)[1], 0)


def _combine(lyr, slot, x1, mods, rf, l2g, l2b, eo):
    bsz, n_tok, _ = x1.shape
    nt = n_tok // TILE
    n_steps = bsz * nt
    tile = lambda s, ly, sl: (s // nt, s % nt, 0)
    lay3 = lambda s, ly, sl: (ly[0], 0, 0)
    return pl.pallas_call(
        functools.partial(_combine_kernel, n_steps=n_steps),
        out_shape=jax.ShapeDtypeStruct((bsz, n_tok, D_MODEL), F32),
        grid_spec=pltpu.PrefetchScalarGridSpec(
            num_scalar_prefetch=2,
            grid=(n_steps,),
            in_specs=[pl.BlockSpec((None, TILE, D_MODEL), tile),
                      pl.BlockSpec((None, None, N_MOD, D_MODEL),
                                   lambda s, ly, sl: (ly[0], jnp.where(s % nt == 0, bsz, s // nt), 0, 0)),
                      pl.BlockSpec((TILE, LANES), lambda s, ly, sl: (s, 0)),
                      pl.BlockSpec((None, 1, D_MODEL), lay3),
                      pl.BlockSpec((None, 1, D_MODEL), lay3),
                      pl.BlockSpec(memory_space=pl.ANY)],
            out_specs=pl.BlockSpec((None, TILE, D_MODEL), tile),
            scratch_shapes=[pltpu.VMEM((2, TOP_K, TILE, D_MODEL // 2), jnp.uint32),
                            pltpu.SemaphoreType.DMA((2, TILE // GATHER_ROWS))],
        ),
        compiler_params=pltpu.CompilerParams(dimension_semantics=("arbitrary",),
                                             vmem_limit_bytes=40 * MIB),
        name="combine",
    )(lyr, slot, x1, mods, rf, l2g, l2b, eo)


def _slot_tables(route_i, counts, n_blocks):
    counts = counts.astype(jnp.int32)
    padded = (counts + EXPERT_BLOCK - 1) // EXPERT_BLOCK * EXPERT_BLOCK
    pad_ends = jnp.cumsum(padded)
    pad_starts = pad_ends - padded
    expert = route_i[:, :TOP_K]
    rank = route_i[:, TOP_K:2 * TOP_K]
    slot = (pad_starts[expert] + rank).reshape(-1)
    block_row = jnp.arange(n_blocks, dtype=jnp.int32) * EXPERT_BLOCK
    block_expert = jnp.minimum(jnp.sum((pad_ends[None, :] <= block_row[:, None]).astype(jnp.int32), axis=1),
                               N_EXPERTS - 1)
    n_used = (pad_ends[-1:] // EXPERT_BLOCK).astype(jnp.int32)
    return slot, block_expert, n_used


def _rope_tables(n_lat):
    rows = n_lat // GRID_W
    row = jnp.repeat(jnp.arange(rows, dtype=F32), GRID_W)
    col = jnp.tile(jnp.arange(GRID_W, dtype=F32), rows)
    axis_dim = HEAD_DIM // 2
    inv_freq = ROPE_THETA ** (-jnp.arange(0, axis_dim, 2, dtype=F32) / axis_dim)
    ang_r = row[:, None] * inv_freq
    ang_c = col[:, None] * inv_freq
    cos = jnp.concatenate([jnp.cos(ang_r)] * 2 + [jnp.cos(ang_c)] * 2, axis=1)
    sin = jnp.concatenate([-jnp.sin(ang_r), jnp.sin(ang_r), -jnp.sin(ang_c), jnp.sin(ang_c)], axis=1)
    cos = jnp.concatenate([jnp.ones((CTX_LEN, HEAD_DIM), F32), cos], axis=0)
    sin = jnp.concatenate([jnp.zeros((CTX_LEN, HEAD_DIM), F32), sin], axis=0)
    return jnp.tile(cos, (1, KV_HEADS)), jnp.tile(sin, (1, KV_HEADS))


def kernel(x, c, ctx, c_ctx, w_mod, b_mod, w_in, b_in, sgu_ln_g, sgu_ln_b, w_spatial, b_spatial, attn_sink, conv_w, conv_b, w_rgate, b_rgate, w_igate, b_igate, lru_lambda, w_proj_a, w_proj_b, w_proj_c, w_out, b_out, ln1_g, ln1_b, ln2_g, ln2_b, w_group, b_group, w_router, b_router, w1, w3, w2):
    bsz, n_lat, _ = x.shape
    n_layers = w_mod.shape[0]
    assert ctx.shape[1] == CTX_LEN == TILE and n_lat % TILE == 0 and bsz < SUBLANES
    n_tok = CTX_LEN + n_lat
    n_blocks = -(-bsz * n_tok * TOP_K // EXPERT_BLOCK) + N_EXPERTS
    n_slots = n_blocks * EXPERT_BLOCK

    cond = jnp.zeros((SUBLANES, D_MODEL), F32).at[:bsz].set(c).at[bsz].set(c_ctx)
    mods = _modulation(cond, w_mod, b_mod).reshape(n_layers, SUBLANES, N_MOD, D_MODEL)
    w_main = w_in[:, :, :MAIN_W].astype(BF16)
    w_ctx = w_in[:, :, MAIN_W:].astype(BF16)
    b_main = b_in[:, None, :MAIN_W]
    b_ctx = b_in[:, None, MAIN_W:]
    w_ri = jnp.concatenate([w_rgate, w_igate], axis=-1).astype(BF16)
    w_sp = w_spatial.astype(BF16)
    b_sp = jnp.repeat(jnp.swapaxes(b_spatial, 1, 2), CHUNK, axis=2)
    wa, wb, wc, wo = (w.astype(BF16) for w in (w_proj_a, w_proj_b, w_proj_c, w_out))
    w_rt = jnp.zeros((n_layers, D_MODEL, LANES), F32)
    w_rt = w_rt.at[:, :, :N_GROUPS].set(w_group).at[:, :, N_GROUPS:N_GROUPS + N_EXPERTS].set(w_router)
    b_rt = jnp.zeros((n_layers, 1, LANES), F32)
    b_rt = b_rt.at[:, 0, :N_GROUPS].set(b_group).at[:, 0, N_GROUPS:N_GROUPS + N_EXPERTS].set(b_router)
    cos_t, sin_t = _rope_tables(n_lat)
    row3 = lambda a: a[:, None, :]
    conv_b4, b_r4, b_i4 = (a[:, :, None, :] for a in (conv_b, b_rgate, b_igate))
    sink, sg, sb, bo = row3(attn_sink), row3(sgu_ln_g), row3(sgu_ln_b), row3(b_out)
    l1g, l1b, l2g, l2b = row3(ln1_g), row3(ln1_b), row3(ln2_g), row3(ln2_b)

    def layer(l, xc):
        lyr = jnp.full((1,), l, jnp.int32)
        k, v, hf, hb = _kv_lru(lyr, xc, mods, w_ctx, b_ctx, cos_t, sin_t, conv_w, conv_b4,
                               w_ri, b_r4, b_i4, lru_lambda)
        x1, h2, route_i, route_f, counts = _mixer(
            lyr, xc, mods, w_main, b_main, cos_t, sin_t, sink, k, v, hf, hb,
            sg, sb, w_sp, b_sp, wa, wb, wc, wo, bo, l1g, l1b, w_rt, b_rt)
        slot, block_expert, n_used = _slot_tables(route_i, counts[0, :N_EXPERTS], n_blocks)
        buf = _dispatch(slot, h2, n_slots)
        eo = _experts(lyr, block_expert, n_used, buf, w1, w3, w2)
        return _combine(lyr, slot, x1, mods, route_f, l2g, l2b, eo)

    xc = jnp.concatenate([ctx, x], axis=1)
    for l in range(n_layers):
        xc = layer(l, xc)
    return xc[:, CTX_LEN:]
```

```python
import functools

import jax
import jax.numpy as jnp
from jax import lax
from jax.experimental import pallas as pl
from jax.experimental.pallas import tpu as pltpu

F32 = jnp.float32
BF16 = jnp.bfloat16

D_MODEL = 1024
DEPTH = 4
GRID_W = 64
CTX_LEN = 256
HEAD_DIM = 64
ATT_HEADS = 8
KV_HEADS = 2
Q_PER_KV = ATT_HEADS // KV_HEADS
ATT_W = ATT_HEADS * HEAD_DIM
KV_W = KV_HEADS * HEAD_DIM
WINDOW = 128
ROPE_THETA = 10000.0
CHUNK = 128
SGU_GROUPS = 4
SGU_W = SGU_GROUPS * CHUNK
LRU_W = D_MODEL
LRU_BLOCKS = 8
LRU_BW = LRU_W // LRU_BLOCKS
CONV_W = 4
LRU_C = 8.0
N_GROUPS = 4
EXPERTS_PER_GROUP = 8
N_EXPERTS = N_GROUPS * EXPERTS_PER_GROUP
TOP_K = 2
D_EXPERT = 512
N_MOD = 6
ALPHA = (2.0 * DEPTH) ** 0.25
LN_EPS = 1e-6
NEG_INF = -1e30

Q0, U0, V0, G0, Z0 = 0, ATT_W, ATT_W + SGU_W, ATT_W + 2 * SGU_W, ATT_W + 2 * SGU_W + 3 * D_MODEL
MAIN_W = Z0 + LRU_W
CTX_W = 2 * KV_W + LRU_W

TILE = 256
SUBLANES = 8
LANES = 128
EXPERT_BLOCK = 512
GATHER_ROWS = 64
MIB = 1024 * 1024
VMEM_LIMIT = 56 * MIB


def _sigmoid(x):
    return 0.5 * (1.0 + jnp.tanh(0.5 * x))


def _gelu(x):
    return 0.5 * x * (1.0 + jnp.tanh(0.7978845608028654 * (x + 0.044715 * (x * x * x))))


def _layer_norm(x, g, b):
    mu = jnp.mean(x, axis=-1, keepdims=True)
    xc = x - mu
    var = jnp.mean(xc * xc, axis=-1, keepdims=True)
    return xc * lax.rsqrt(var + LN_EPS) * g + b


def _dot(a, b):
    return jnp.dot(a, b, preferred_element_type=F32)


def _dot3(a, b):
    a_hi = a.astype(BF16)
    a_lo = (a - a_hi.astype(F32)).astype(BF16)
    b_hi = b.astype(BF16)
    b_lo = (b - b_hi.astype(F32)).astype(BF16)
    return _dot(a_hi, b_hi) + (_dot(a_hi, b_lo) + _dot(a_lo, b_hi))


def _pack_rows(v):
    n = v.shape[1] // 2
    bits = pltpu.bitcast(v.astype(BF16).astype(F32), jnp.uint32)
    return (bits[:, :n] >> 16) | bits[:, n:]


def _unpack_rows(p):
    lo = pltpu.bitcast(p << 16, F32)
    hi = pltpu.bitcast(p & jnp.uint32(0xFFFF0000), F32)
    return jnp.concatenate([lo, hi], axis=1)


def _rope(x, cos, sin_signed):
    n = x.shape[1] // LANES
    if n > 1:
        cos = jnp.concatenate([cos] * n, axis=1)
        sin_signed = jnp.concatenate([sin_signed] * n, axis=1)
    lane = lax.broadcasted_iota(jnp.int32, x.shape, 1)
    first = (lane & 31) < 16
    w = x.shape[1]
    partner = jnp.where(first, pltpu.roll(x, w - 16, 1), pltpu.roll(x, 16, 1))
    return x * cos + partner * sin_signed


def _mods_kernel(c_ref, w_ref, b_ref, o_ref):
    c = c_ref[...]
    o_ref[...] = _dot3(c * _sigmoid(c), w_ref[...]) + b_ref[...]


def _modulation(cond, w_mod, b_mod):
    n_layers = w_mod.shape[0]
    tn = 1536
    return pl.pallas_call(
        _mods_kernel,
        out_shape=jax.ShapeDtypeStruct((n_layers, SUBLANES, N_MOD * D_MODEL), F32),
        grid=(n_layers, N_MOD * D_MODEL // tn),
        in_specs=[
            pl.BlockSpec((SUBLANES, D_MODEL), lambda l, j: (0, 0)),
            pl.BlockSpec((None, D_MODEL, tn), lambda l, j: (l, 0, j)),
            pl.BlockSpec((None, 1, tn), lambda l, j: (l, 0, j)),
        ],
        out_specs=pl.BlockSpec((None, SUBLANES, tn), lambda l, j: (l, 0, j)),
        compiler_params=pltpu.CompilerParams(dimension_semantics=("arbitrary", "arbitrary"),
                                             vmem_limit_bytes=40 * MIB),
        name="modulation",
    )(cond, w_mod, b_mod.reshape(n_layers, 1, N_MOD * D_MODEL))


def _scan_tile(a_ref, u_ref, carry_ref, out_ref, reverse):
    width = a_ref.shape[1]
    row = lax.broadcasted_iota(jnp.int32, (SUBLANES, width), 0)
    n_blk = a_ref.shape[0] // SUBLANES

    def body(j, carry):
        blk = (n_blk - 1 - j) if reverse else j
        r0 = pl.multiple_of(blk * SUBLANES, SUBLANES)
        a = a_ref[pl.ds(r0, SUBLANES), :]
        u = u_ref[pl.ds(r0, SUBLANES), :]
        for s in (1, 2, 4):
            if reverse:
                keep = row < SUBLANES - s
                a_sh = jnp.where(keep, pltpu.roll(a, SUBLANES - s, 0), 1.0)
                u_sh = jnp.where(keep, pltpu.roll(u, SUBLANES - s, 0), 0.0)
            else:
                keep = row >= s
                a_sh = jnp.where(keep, pltpu.roll(a, s, 0), 1.0)
                u_sh = jnp.where(keep, pltpu.roll(u, s, 0), 0.0)
            u = a * u_sh + u
            a = a * a_sh
        h = u + a * carry
        out_ref[pl.ds(r0, SUBLANES), :] = h
        return h[0:1, :] if reverse else h[SUBLANES - 1:SUBLANES, :]

    carry_ref[...] = lax.fori_loop(0, n_blk, body, carry_ref[...])


def _lru_inputs(t, wri_ref, br_ref, bi_ref, sp, a_ref, u_ref):
    for n in range(LRU_BLOCKS):
        cols = slice(n * LRU_BW, (n + 1) * LRU_BW)
        tn = t[:, cols]
        g = _dot(tn.astype(BF16), wri_ref[n])
        r = _sigmoid(g[:, :LRU_BW] + br_ref[:, cols])
        i = _sigmoid(g[:, LRU_BW:] + bi_ref[:, cols])
        a = jnp.exp((-LRU_C) * r * sp[:, cols])
        a_ref[:, cols] = a
        u_ref[:, cols] = jnp.sqrt(1.0 - a * a) * i * tn


def _kv_lru_kernel(ly_ref, xf_ref, xb_ref, mod_ref, wc_ref, bc_ref, cos_ref, sin_ref,
                   cw_ref, cb_ref, wri_ref, br_ref, bi_ref, lam_ref,
                   k_ref, v_ref, hf_ref, hb_ref,
                   xsf, xsb, af, uf, ab, ub, cf, cbk):
    del ly_ref
    i = pl.program_id(1)
    m = mod_ref[...]
    shift, scale = m[0:1, :], 1.0 + m[1:2, :]

    @pl.when(i == 0)
    def _():
        cf[...] = jnp.zeros_like(cf)
        cbk[...] = jnp.zeros_like(cbk)
        xsf[...] = jnp.zeros_like(xsf)
        xsb[...] = jnp.zeros_like(xsb)

    h = (xf_ref[...] * scale + shift).astype(BF16)
    p = _dot(h, wc_ref[...]) + bc_ref[...]
    k_ref[...] = _rope(p[:, :KV_W], cos_ref[...], sin_ref[...]).astype(BF16)
    v_ref[...] = p[:, KV_W:2 * KV_W].astype(BF16)
    xr_f = p[:, 2 * KV_W:]
    h = (xb_ref[...] * scale + shift).astype(BF16)
    xr_b = _dot(h, wc_ref[:, 2 * KV_W:]) + bc_ref[:, 2 * KV_W:]

    inside = jnp.broadcast_to(i > 1, (SUBLANES, LRU_W))
    xsf[0:SUBLANES, :] = jnp.where(inside, xsf[TILE:TILE + SUBLANES, :], 0.0)
    xsb[TILE:TILE + SUBLANES, :] = jnp.where(inside, xsb[0:SUBLANES, :], 0.0)

    xsf[SUBLANES:SUBLANES + TILE, :] = xr_f
    xsb[0:TILE, :] = xr_b

    lam = lam_ref[...]
    sp = jnp.maximum(-lam, 0.0) + jnp.log1p(jnp.exp(-jnp.abs(lam)))

    cw = cw_ref[0]
    t = (cb_ref[0] + cw[0:1, :] * xsf[SUBLANES - 3:SUBLANES - 3 + TILE, :]
         + cw[1:2, :] * xsf[SUBLANES - 2:SUBLANES - 2 + TILE, :]
         + cw[2:3, :] * xsf[SUBLANES - 1:SUBLANES - 1 + TILE, :]
         + cw[3:4, :] * xr_f)
    _lru_inputs(t, wri_ref.at[0], br_ref.at[0], bi_ref.at[0], sp[0:1, :], af, uf)
    cw = cw_ref[1]
    t = (cb_ref[1] + cw[0:1, :] * xr_b
         + cw[1:2, :] * xsb[1:1 + TILE, :]
         + cw[2:3, :] * xsb[2:2 + TILE, :]
         + cw[3:4, :] * xsb[3:3 + TILE, :])
    _lru_inputs(t, wri_ref.at[1], br_ref.at[1], bi_ref.at[1], sp[1:2, :], ab, ub)

    _scan_tile(af, uf, cf, hf_ref, reverse=False)
    _scan_tile(ab, ub, cbk, hb_ref, reverse=True)


def _kv_lru(lyr, xc, mods, w_ctx, b_ctx, cos_t, sin_t, conv_w, conv_b, w_ri, b_r, b_i, lam):
    bsz, n_tok, _ = xc.shape
    nt = n_tok // TILE
    tile_f = lambda b, i, ly: (b, i, 0)
    tile_b = lambda b, i, ly: (b, jnp.where(i == 0, 0, nt - i), 0)
    lay3 = lambda b, i, ly: (ly[0], 0, 0)
    lay4 = lambda b, i, ly: (ly[0], 0, 0, 0)
    return pl.pallas_call(
        _kv_lru_kernel,
        out_shape=(jax.ShapeDtypeStruct((bsz, n_tok, KV_W), BF16),
                   jax.ShapeDtypeStruct((bsz, n_tok, KV_W), BF16),
                   jax.ShapeDtypeStruct((bsz, n_tok, LRU_W), F32),
                   jax.ShapeDtypeStruct((bsz, n_tok, LRU_W), F32)),
        grid_spec=pltpu.PrefetchScalarGridSpec(
            num_scalar_prefetch=1,
            grid=(bsz, nt),
            in_specs=[
                pl.BlockSpec((None, TILE, D_MODEL), tile_f),
                pl.BlockSpec((None, TILE, D_MODEL), tile_b),
                pl.BlockSpec((None, None, N_MOD, D_MODEL),
                             lambda b, i, ly: (ly[0], jnp.where(i == 0, bsz, b), 0, 0)),
                pl.BlockSpec((None, D_MODEL, CTX_W), lay3),
                pl.BlockSpec((None, 1, CTX_W), lay3),
                pl.BlockSpec((TILE, KV_W), lambda b, i, ly: (i, 0)),
                pl.BlockSpec((TILE, KV_W), lambda b, i, ly: (i, 0)),
                pl.BlockSpec((None, 2, CONV_W, LRU_W), lay4),
                pl.BlockSpec((None, 2, 1, LRU_W), lay4),
                pl.BlockSpec((None, 2, LRU_BLOCKS, LRU_BW, 2 * LRU_BW), lambda b, i, ly: (ly[0], 0, 0, 0, 0)),
                pl.BlockSpec((None, 2, 1, LRU_W), lay4),
                pl.BlockSpec((None, 2, 1, LRU_W), lay4),
                pl.BlockSpec((None, 2, LRU_W), lay3),
            ],
            out_specs=(pl.BlockSpec((None, TILE, KV_W), tile_f),
                       pl.BlockSpec((None, TILE, KV_W), tile_f),
                       pl.BlockSpec((None, TILE, LRU_W), tile_f),
                       pl.BlockSpec((None, TILE, LRU_W), tile_b)),
            scratch_shapes=[pltpu.VMEM((TILE + SUBLANES, LRU_W), F32),
                            pltpu.VMEM((TILE + SUBLANES, LRU_W), F32),
                            pltpu.VMEM((TILE, LRU_W), F32), pltpu.VMEM((TILE, LRU_W), F32),
                            pltpu.VMEM((TILE, LRU_W), F32), pltpu.VMEM((TILE, LRU_W), F32),
                            pltpu.VMEM((1, LRU_W), F32), pltpu.VMEM((1, LRU_W), F32)],
        ),
        compiler_params=pltpu.CompilerParams(dimension_semantics=("arbitrary", "arbitrary"),
                                             vmem_limit_bytes=VMEM_LIMIT),
        name="kv_lru",
    )(lyr, xc, xc, mods, w_ctx, b_ctx, cos_t, sin_t, conv_w, conv_b, w_ri, b_r, b_i, lam)


def _attn_group(q, g, k_all, v_all, bias, sink):
    rows = q.shape[0]
    lane = lax.broadcasted_iota(jnp.int32, (rows, LANES), 1)
    mine = (lane >= g * HEAD_DIM) & (lane < (g + 1) * HEAD_DIM)
    parts = []
    for j in range(Q_PER_KV):
        hd = g * Q_PER_KV + j
        t = q[:, (hd // 2) * LANES:(hd // 2 + 1) * LANES]
        if hd % 2 != g:
            t = pltpu.roll(t, HEAD_DIM, 1)
        parts.append(jnp.where(mine, t, 0.0))
    qs = jnp.concatenate(parts, axis=0).astype(BF16)
    s = lax.dot_general(qs, k_all, (((1,), (1,)), ((), ())), preferred_element_type=F32)
    s = s + bias
    blk = lax.broadcasted_iota(jnp.int32, (Q_PER_KV * rows, 1), 0)
    sink_col = jnp.zeros((Q_PER_KV * rows, 1), F32)
    for j in range(Q_PER_KV):
        hd = g * Q_PER_KV + j
        sink_col = jnp.where((blk >= j * rows) & (blk < (j + 1) * rows), sink[:, hd:hd + 1], sink_col)
    mx = jnp.maximum(jnp.max(s, axis=-1, keepdims=True), sink_col)
    p = jnp.exp(s - mx)
    den = jnp.sum(p, axis=-1, keepdims=True) + jnp.exp(sink_col - mx)
    o = _dot(p.astype(BF16), v_all)
    return o * (1.0 / den)


def _merge_heads(rows, outs):
    lane = lax.broadcasted_iota(jnp.int32, (rows, LANES), 1)
    low = lane < HEAD_DIM
    tiles = []
    for t in range(ATT_HEADS // 2):
        g = (2 * t) // Q_PER_KV
        j = 2 * t - g * Q_PER_KV
        even = outs[g][j * rows:(j + 1) * rows, :]
        odd = outs[g][(j + 1) * rows:(j + 2) * rows, :]
        if g == 1:
            even = pltpu.roll(even, HEAD_DIM, 1)
        else:
            odd = pltpu.roll(odd, HEAD_DIM, 1)
        tiles.append(jnp.where(low, even, odd))
    return jnp.concatenate(tiles, axis=1)


def _mixer_kernel(ly_ref, x_ref, mod_ref, w_ref, b_ref, cos_ref, sin_ref, sink_ref, bias0_ref, bias1_ref,
                  kc_ref, vc_ref, kp_ref, vp_ref, kt_ref, vt_ref, kn_ref, vn_ref,
                  hf_ref, hb_ref, sg_ref, sb_ref, ws_ref, bs_ref,
                  wa_ref, wb_ref, wc_ref, wo_ref, bo_ref, l1g_ref, l1b_ref, wrt_ref, brt_ref,
                  x1_ref, h2_ref, ri_ref, rf_ref, cnt_ref,
                  run_s):
    del ly_ref
    @pl.when((pl.program_id(0) == 0) & (pl.program_id(1) == 0))
    def _():
        run_s[...] = jnp.zeros_like(run_s)

    x = x_ref[...]
    m = mod_ref[...]
    h = (x * (1.0 + m[1:2, :]) + m[0:1, :]).astype(BF16)

    def proj(c0, width):
        return _dot(h, w_ref[:, c0:c0 + width]) + b_ref[:, c0:c0 + width]

    q = _rope(proj(Q0, ATT_W), cos_ref[...], sin_ref[...]) * (HEAD_DIM ** -0.5)
    sink = sink_ref[...]

    kt, vt = kt_ref[...], vt_ref[...]
    att = []
    for half in range(2):
        if half == 0:
            k_all = jnp.concatenate([kc_ref[...], kp_ref[...], kt], axis=0)
            v_all = jnp.concatenate([vc_ref[...], vp_ref[...], vt], axis=0)
            bias = bias0_ref[...]
        else:
            k_all = jnp.concatenate([kc_ref[...], kt, kn_ref[...]], axis=0)
            v_all = jnp.concatenate([vc_ref[...], vt, vn_ref[...]], axis=0)
            bias = bias1_ref[...]
        qh = q[half * WINDOW:(half + 1) * WINDOW, :]
        outs = [_attn_group(qh, g, k_all, v_all, bias, sink) for g in range(KV_HEADS)]
        att.append(_merge_heads(WINDOW, outs))
    b_br = jnp.concatenate(att, axis=0)

    u = _gelu(proj(U0, SGU_W))
    vn = _layer_norm(_gelu(proj(V0, SGU_W)), sg_ref[...], sb_ref[...]).astype(BF16)
    mixed = []
    for c in range(TILE // CHUNK):
        row = []
        for g in range(SGU_GROUPS):
            row.append(_dot(ws_ref[g], vn[c * CHUNK:(c + 1) * CHUNK, g * CHUNK:(g + 1) * CHUNK]))
        mixed.append(jnp.concatenate(row, axis=1) + bs_ref[...])
    a_br = u * jnp.concatenate(mixed, axis=0)

    r_br = (hf_ref[...] + hb_ref[...]) * _gelu(proj(Z0, LRU_W))

    y = _sigmoid(proj(G0, D_MODEL)) * _dot(a_br.astype(BF16), wa_ref[...])
    y = y + _sigmoid(proj(G0 + D_MODEL, D_MODEL)) * _dot(b_br.astype(BF16), wb_ref[...])
    y = y + _sigmoid(proj(G0 + 2 * D_MODEL, D_MODEL)) * _dot(r_br.astype(BF16), wc_ref[...])
    out = _dot(y.astype(BF16), wo_ref[...]) + bo_ref[...]
    x1 = _layer_norm(ALPHA * x + m[2:3, :] * out, l1g_ref[...], l1b_ref[...])
    x1_ref[...] = x1
    h2 = x1 * (1.0 + m[4:5, :]) + m[3:4, :]
    h2_ref[...] = _pack_rows(h2)

    logits = _dot3(h2, wrt_ref[...]) + brt_ref[...]
    lane = lax.broadcasted_iota(jnp.int32, (TILE, LANES), 1)
    lane_f = lane.astype(F32)
    big = float(LANES)
    gl = jnp.where(lane < N_GROUPS, logits, NEG_INF)
    gmax = jnp.max(gl, axis=-1, keepdims=True)
    g_idx = jnp.min(jnp.where(gl == gmax, lane_f, big), axis=-1, keepdims=True)
    g_w = 1.0 / jnp.sum(jnp.exp(gl - gmax), axis=-1, keepdims=True)
    lo = N_GROUPS + EXPERTS_PER_GROUP * g_idx
    el = jnp.where((lane_f >= lo) & (lane_f < lo + EXPERTS_PER_GROUP), logits, NEG_INF)
    v1 = jnp.max(el, axis=-1, keepdims=True)
    l1 = jnp.min(jnp.where(el == v1, lane_f, big), axis=-1, keepdims=True)
    el2 = jnp.where(lane_f == l1, NEG_INF, el)
    v2 = jnp.max(el2, axis=-1, keepdims=True)
    l2 = jnp.min(jnp.where(el2 == v2, lane_f, big), axis=-1, keepdims=True)
    e21 = jnp.exp(v2 - v1)
    gate1 = g_w / (1.0 + e21)
    gate2 = g_w * e21 / (1.0 + e21)
    e1 = l1 - N_GROUPS
    e2 = l2 - N_GROUPS

    oh1 =jnp.where(lane_f == e1, 1.0, 0.0)
    oh2 = jnp.where(lane_f == e2, 1.0, 0.0)
    both = oh1 + oh2
    tr = lax.broadcasted_iota(jnp.int32, (TILE, TILE), 0)
    tc = lax.broadcasted_iota(jnp.int32, (TILE, TILE), 1)
    earlier = jnp.where(tc < tr, 1.0, 0.0).astype(BF16)
    before = _dot(earlier, both.astype(BF16)) + run_s[...]
    rank1 = jnp.sum(before * oh1, axis=-1, keepdims=True)
    rank2 = jnp.sum(before * oh2, axis=-1, keepdims=True)
    run_s[...] = run_s[...] + jnp.sum(both, axis=0, keepdims=True)
    cnt_ref[...] = jnp.broadcast_to(run_s[...], cnt_ref.shape)

    ri = jnp.where(lane == 0, e1, jnp.where(lane == 1, e2, jnp.where(lane == 2, rank1,
                                                                      jnp.where(lane == 3, rank2, 0.0))))
    ri_ref[...] = ri.astype(jnp.int32)
    rf_ref[...] = jnp.where(lane == 0, gate1, jnp.where(lane == 1, gate2, 0.0))


def _window_bias():
    r = jnp.arange(Q_PER_KV * WINDOW)[:, None] % WINDOW
    c = jnp.arange(CTX_LEN + 3 * WINDOW)[None, :]
    j_prev = c - CTX_LEN
    j_next = c - (CTX_LEN + 2 * WINDOW)
    in_prev = (j_prev >= 0) & (j_prev < WINDOW)
    in_next = j_next >= 0
    full = ~(in_prev & (j_prev < r)) & ~(in_next & (j_next > r))
    ctx_only = jnp.broadcast_to(c < CTX_LEN, full.shape)
    masks = jnp.stack([jnp.stack([full, full & ~in_prev, ctx_only]),
                       jnp.stack([full, full & ~in_next, ctx_only])])
    return jnp.where(masks, 0.0, NEG_INF).astype(F32)


def _mixer(lyr, xc, mods, w_main, b_main, cos_t, sin_t, sink, k, v, hf, hb,
           sg, sb, w_sp, b_sp, wa, wb, wc, wo, bo, l1g, l1b, w_rt, b_rt):
    bsz, n_tok, _ = xc.shape
    nt = n_tok // TILE
    n_half = n_tok // WINDOW
    once = pl.Buffered(1)
    bias = _window_bias()
    n_q, n_keys = bias.shape[2:]
    variant = lambda i, edge: jnp.where(i == 0, 2, jnp.where(i == edge, 1, 0))
    bias0 = pl.BlockSpec((None, None, n_q, n_keys), lambda b, i, ly: (0, variant(i, 1), 0, 0))
    bias1 = pl.BlockSpec((None, None, n_q, n_keys), lambda b, i, ly: (1, variant(i, nt - 1), 0, 0))
    tile = lambda b, i, ly: (b, i, 0)
    lay3 = lambda b, i, ly: (ly[0], 0, 0)
    flat = lambda b, i, ly: (b * nt + i, 0)
    kv_ctx = pl.BlockSpec((None, CTX_LEN, KV_W), lambda b, i, ly: (b, 0, 0))
    kv_prev = pl.BlockSpec((None, WINDOW, KV_W), lambda b, i, ly: (b, jnp.maximum(2 * i - 1, 0), 0))
    kv_this = pl.BlockSpec((None, TILE, KV_W), tile)
    kv_next = pl.BlockSpec((None, WINDOW, KV_W), lambda b, i, ly: (b, jnp.minimum(2 * i + 2, n_half - 1), 0))
    row_vec = lambda width: pl.BlockSpec((None, 1, width), lay3)
    return pl.pallas_call(
        _mixer_kernel,
        out_shape=(jax.ShapeDtypeStruct((bsz, n_tok, D_MODEL), F32),
                   jax.ShapeDtypeStruct((bsz * n_tok, D_MODEL // 2), jnp.uint32),
                   jax.ShapeDtypeStruct((bsz * n_tok, LANES), jnp.int32),
                   jax.ShapeDtypeStruct((bsz * n_tok, LANES), F32),
                   jax.ShapeDtypeStruct((SUBLANES, LANES), F32)),
        grid_spec=pltpu.PrefetchScalarGridSpec(
            num_scalar_prefetch=1,
            grid=(bsz, nt),
            in_specs=[
                pl.BlockSpec((None, TILE, D_MODEL), tile),
                pl.BlockSpec((None, None, N_MOD, D_MODEL),
                             lambda b, i, ly: (ly[0], jnp.where(i == 0, bsz, b), 0, 0)),
                pl.BlockSpec((None, D_MODEL, MAIN_W), lay3, pipeline_mode=once),
                row_vec(MAIN_W),
                pl.BlockSpec((TILE, KV_W), lambda b, i, ly: (i, 0)),
                pl.BlockSpec((TILE, KV_W), lambda b, i, ly: (i, 0)),
                pl.BlockSpec((None, 1, ATT_HEADS), lay3),
                bias0, bias1,
                kv_ctx, kv_ctx, kv_prev, kv_prev, kv_this, kv_this, kv_next, kv_next,
                pl.BlockSpec((None, TILE, LRU_W), tile),
                pl.BlockSpec((None, TILE, LRU_W), tile),
                row_vec(SGU_W), row_vec(SGU_W),
                pl.BlockSpec((None, SGU_GROUPS, CHUNK, CHUNK), lambda b, i, ly: (ly[0], 0, 0, 0)),
                pl.BlockSpec((None, CHUNK, SGU_W), lay3),
                pl.BlockSpec((None, SGU_W, D_MODEL), lay3, pipeline_mode=once),
                pl.BlockSpec((None, ATT_W, D_MODEL), lay3, pipeline_mode=once),
                pl.BlockSpec((None, LRU_W, D_MODEL), lay3, pipeline_mode=once),
                pl.BlockSpec((None, D_MODEL, D_MODEL), lay3, pipeline_mode=once),
                row_vec(D_MODEL), row_vec(D_MODEL), row_vec(D_MODEL),
                pl.BlockSpec((None, D_MODEL, LANES), lay3),
                row_vec(LANES),
            ],
            out_specs=(pl.BlockSpec((None, TILE, D_MODEL), tile),
                       pl.BlockSpec((TILE, D_MODEL // 2), flat),
                       pl.BlockSpec((TILE, LANES), flat),
                       pl.BlockSpec((TILE, LANES), flat),
                       pl.BlockSpec((SUBLANES, LANES), lambda b, i, ly: (0, 0))),
            scratch_shapes=[pltpu.VMEM((1, LANES), F32)],
        ),
        compiler_params=pltpu.CompilerParams(dimension_semantics=("arbitrary", "arbitrary"),
                                             vmem_limit_bytes=VMEM_LIMIT),
        name="mixer",
    )(lyr, xc, mods, w_main, b_main, cos_t, sin_t, sink, bias, bias, k, v, k, v, k, v, k, v, hf, hb,
      sg, sb, w_sp, b_sp, wa, wb, wc, wo, bo, l1g, l1b, w_rt, b_rt)


def _dispatch_kernel(slot_ref, h2_ref, buf_in_ref, buf_ref, sem):
    del buf_in_ref
    base = pl.program_id(0) * TILE

    def row_copy(t, k):
        return pltpu.make_async_copy(h2_ref.at[pl.ds(t, 1)],
                                     buf_ref.at[pl.ds(slot_ref[TOP_K * (base + t) + k], 1)], sem)

    def start(t, carry):
        for k in range(TOP_K):
            row_copy(t, k).start(priority=k)
        return carry

    def wait(t, carry):
        for k in range(TOP_K):
            row_copy(t, k).wait()
        return carry

    lax.fori_loop(0, TILE, start, 0, unroll=8)
    lax.fori_loop(0, TILE, wait, 0, unroll=8)


def _dispatch(slot, h2, n_slots):
    n_tok, width = h2.shape
    return pl.pallas_call(
        _dispatch_kernel,
        out_shape=jax.ShapeDtypeStruct((n_slots, width), h2.dtype),
        grid_spec=pltpu.PrefetchScalarGridSpec(
            num_scalar_prefetch=1,
            grid=(n_tok // TILE,),
            in_specs=[pl.BlockSpec((TILE, width), lambda i, sl: (i, 0)),
                      pl.BlockSpec(memory_space=pl.ANY)],
            out_specs=pl.BlockSpec(memory_space=pl.ANY),
            scratch_shapes=[pltpu.SemaphoreType.DMA(())],
        ),
        input_output_aliases={2: 0},
        compiler_params=pltpu.CompilerParams(dimension_semantics=("arbitrary",)),
        name="dispatch",
    )(slot, h2, jnp.zeros((n_slots, width), h2.dtype))


def _experts_kernel(ly_ref, be_ref, nu_ref, x_ref, w1_ref, w3_ref, w2_ref, o_ref, w1s, w3s, w2s):
    del ly_ref
    i = pl.program_id(0)
    changed = (i == 0) | (be_ref[i] != be_ref[jnp.maximum(i - 1, 0)])

    @pl.when((i < nu_ref[0]) & changed)
    def _():
        w1s[...] = w1_ref[...].astype(BF16)
        w3s[...] = w3_ref[...].astype(BF16)
        w2s[...] = w2_ref[...].astype(BF16)

    @pl.when(i < nu_ref[0])
    def _():
        x = _unpack_rows(x_ref[...]).astype(BF16)
        h1 = _dot(x, w1s[...])
        h3 = _dot(x, w3s[...])
        act = (h1 * _sigmoid(h1) * h3).astype(BF16)
        o_ref[...] = _pack_rows(_dot(act, w2s[...]))

    @pl.when(i >= nu_ref[0])
    def _():
        o_ref[...] = jnp.zeros_like(o_ref)


def _experts(lyr, block_expert, n_used, buf, w1, w3, w2):
    n_blocks = buf.shape[0] // EXPERT_BLOCK

    def blk(i, ly, be, nu):
        return (jnp.minimum(i, nu[0] - 1), 0)

    def wsel(i, ly, be, nu):
        return (ly[0], be[jnp.minimum(i, nu[0] - 1)], 0, 0)

    return pl.pallas_call(
        _experts_kernel,
        out_shape=jax.ShapeDtypeStruct(buf.shape, buf.dtype),
        grid_spec=pltpu.PrefetchScalarGridSpec(
            num_scalar_prefetch=3,
            grid=(n_blocks,),
            in_specs=[pl.BlockSpec((EXPERT_BLOCK, D_MODEL // 2), blk),
                      pl.BlockSpec((None, None, D_MODEL, D_EXPERT), wsel),
                      pl.BlockSpec((None, None, D_MODEL, D_EXPERT), wsel),
                      pl.BlockSpec((None, None, D_EXPERT, D_MODEL), wsel)],
            out_specs=pl.BlockSpec((EXPERT_BLOCK, D_MODEL // 2), lambda i, ly, be, nu: (i, 0)),
            scratch_shapes=[pltpu.VMEM((D_MODEL, D_EXPERT), BF16),
                            pltpu.VMEM((D_MODEL, D_EXPERT), BF16),
                            pltpu.VMEM((D_EXPERT, D_MODEL), BF16)],
        ),
        compiler_params=pltpu.CompilerParams(dimension_semantics=("arbitrary",),
                                             vmem_limit_bytes=40 * MIB),
        name="experts",
    )(lyr, block_expert, n_used, buf, w1, w3, w2)


def _combine_kernel(ly_ref, slot_ref, x1_ref, mod_ref, rf_ref, g_ref, b_ref, eo_ref, o_ref, gbuf, sems,
                    *, n_steps):
    del ly_ref
    s = pl.program_id(0)
    par = s % 2
    n_groups = TILE // GATHER_ROWS
    gate2 = mod_ref[5:6, :]
    ln_g, ln_b = g_ref[...], b_ref[...]

    def group_copies(tile, buf, j):
        copies = []
        for r in range(GATHER_ROWS):
            t = j * GATHER_ROWS + r
            for k in range(TOP_K):
                src = eo_ref.at[pl.ds(slot_ref[TOP_K * (tile * TILE + t) + k], 1)]
                copies.append((k, pltpu.make_async_copy(src, gbuf.at[buf, k, pl.ds(t, 1)], sems.at[buf, j])))
        return copies

    def start_group(tile, buf, j):
        for k, cp in group_copies(tile, buf, j):
            cp.start(priority=k)

    def wait_group(tile, buf, j):
        for _, cp in group_copies(tile, buf, j):
            cp.wait()

    @pl.when(s == 0)
    def _():
        lax.fori_loop(0, n_groups, lambda j, c: (start_group(s, par, j), c)[1], 0)

    nxt = jnp.minimum(s + 1, n_steps - 1)

    def body(j, carry):
        start_group(nxt, 1 - par, j)
        wait_group(s, par, j)
        rows = pl.ds(pl.multiple_of(j * GATHER_ROWS, GATHER_ROWS), GATHER_ROWS)
        rf = rf_ref[rows, :]
        y = (rf[:, 0:1] * _unpack_rows(gbuf[par, 0, rows, :])
             + rf[:, 1:2] * _unpack_rows(gbuf[par, 1, rows, :]))
        o_ref[rows, :] = _layer_norm(ALPHA * x1_ref[rows, :] + gate2 * y, ln_g, ln_b)
        return carry

    lax.fori_loop(0, n_groups, body, 0)

    @pl.when(s == n_steps - 1)
    def _():
        lax.fori_loop(0, n_groups, lambda j, c: (wait_group(nxt, 1 - par, j), c)[1], 0)


def _combine(lyr, slot, x1, mods, rf, l2g, l2b, eo):
    bsz, n_tok, _ = x1.shape
    nt = n_tok // TILE
    n_steps = bsz * nt
    tile = lambda s, ly, sl: (s // nt, s % nt, 0)
    lay3 = lambda s, ly, sl: (ly[0], 0, 0)
    return pl.pallas_call(
        functools.partial(_combine_kernel, n_steps=n_steps),
        out_shape=jax.ShapeDtypeStruct((bsz, n_tok, D_MODEL), F32),
        grid_spec=pltpu.PrefetchScalarGridSpec(
            num_scalar_prefetch=2,
            grid=(n_steps,),
            in_specs=[pl.BlockSpec((None, TILE, D_MODEL), tile),
                      pl.BlockSpec((None, None, N_MOD, D_MODEL),
                                   lambda s, ly, sl: (ly[0], jnp.where(s % nt == 0, bsz, s // nt), 0, 0)),
                      pl.BlockSpec((TILE, LANES), lambda s, ly, sl: (s, 0)),
                      pl.BlockSpec((None, 1, D_MODEL), lay3),
                      pl.BlockSpec((None, 1, D_MODEL), lay3),
                      pl.BlockSpec(memory_space=pl.ANY)],
            out_specs=pl.BlockSpec((None, TILE, D_MODEL), tile),
            scratch_shapes=[pltpu.VMEM((2, TOP_K, TILE, D_MODEL // 2), jnp.uint32),
                            pltpu.SemaphoreType.DMA((2, TILE // GATHER_ROWS))],
        ),
        compiler_params=pltpu.CompilerParams(dimension_semantics=("arbitrary",),
                                             vmem_limit_bytes=40 * MIB),
        name="combine",
    )(lyr, slot, x1, mods, rf, l2g, l2b, eo)


def _slot_tables(route_i, counts, n_blocks):
    counts = counts.astype(jnp.int32)
    padded = (counts + EXPERT_BLOCK - 1) // EXPERT_BLOCK * EXPERT_BLOCK
    pad_ends = jnp.cumsum(padded)
    pad_starts = pad_ends - padded
    expert = route_i[:, :TOP_K]
    rank = route_i[:, TOP_K:2 * TOP_K]
    is_e = expert[..., None] == jnp.arange(N_EXPERTS, dtype=jnp.int32)
    slot = (jnp.sum(jnp.where(is_e, pad_starts, 0), axis=-1) + rank).reshape(-1)
    block_row = jnp.arange(n_blocks, dtype=jnp.int32) * EXPERT_BLOCK
    block_expert = jnp.minimum(jnp.sum((pad_ends[None, :] <= block_row[:, None]).astype(jnp.int32), axis=1),
                               N_EXPERTS - 1)
    n_used = (pad_ends[-1:] // EXPERT_BLOCK).astype(jnp.int32)
    return slot, block_expert, n_used


def _rope_tables(n_lat):
    rows = n_lat // GRID_W
    row = jnp.repeat(jnp.arange(rows, dtype=F32), GRID_W)
    col = jnp.tile(jnp.arange(GRID_W, dtype=F32), rows)
    axis_dim = HEAD_DIM // 2
    inv_freq = ROPE_THETA ** (-jnp.arange(0, axis_dim, 2, dtype=F32) / axis_dim)
    ang_r = row[:, None] * inv_freq
    ang_c = col[:, None] * inv_freq
    cos = jnp.concatenate([jnp.cos(ang_r)] * 2 + [jnp.cos(ang_c)] * 2, axis=1)
    sin = jnp.concatenate([-jnp.sin(ang_r), jnp.sin(ang_r), -jnp.sin(ang_c), jnp.sin(ang_c)], axis=1)
    cos = jnp.concatenate([jnp.ones((CTX_LEN, HEAD_DIM), F32), cos], axis=0)
    sin = jnp.concatenate([jnp.zeros((CTX_LEN, HEAD_DIM), F32), sin], axis=0)
    return jnp.tile(cos, (1, KV_HEADS)), jnp.tile(sin, (1, KV_HEADS))


def kernel(x, c, ctx, c_ctx, w_mod, b_mod, w_in, b_in, sgu_ln_g, sgu_ln_b, w_spatial, b_spatial, attn_sink, conv_w, conv_b, w_rgate, b_rgate, w_igate, b_igate, lru_lambda, w_proj_a, w_proj_b, w_proj_c, w_out, b_out, ln1_g, ln1_b, ln2_g, ln2_b, w_group, b_group, w_router, b_router, w1, w3, w2):
    bsz, n_lat, _ = x.shape
    n_layers = w_mod.shape[0]
    assert ctx.shape[1] == CTX_LEN == TILE and n_lat % TILE == 0 and bsz < SUBLANES
    n_tok = CTX_LEN + n_lat
    n_blocks = -(-bsz * n_tok * TOP_K // EXPERT_BLOCK) + N_EXPERTS
    n_slots = n_blocks * EXPERT_BLOCK

    cond = jnp.zeros((SUBLANES, D_MODEL), F32).at[:bsz].set(c).at[bsz].set(c_ctx)
    mods = _modulation(cond, w_mod, b_mod).reshape(n_layers, SUBLANES, N_MOD, D_MODEL)
    w_main = w_in[:, :, :MAIN_W].astype(BF16)
    w_ctx = w_in[:, :, MAIN_W:].astype(BF16)
    b_main = b_in[:, None, :MAIN_W]
    b_ctx = b_in[:, None, MAIN_W:]
    w_ri = jnp.concatenate([w_rgate, w_igate], axis=-1).astype(BF16)
    w_sp = w_spatial.astype(BF16)
    b_sp = jnp.repeat(jnp.swapaxes(b_spatial, 1, 2), CHUNK, axis=2)
    wa, wb, wc, wo = (w.astype(BF16) for w in (w_proj_a, w_proj_b, w_proj_c, w_out))
    w_rt = jnp.zeros((n_layers, D_MODEL, LANES), F32)
    w_rt = w_rt.at[:, :, :N_GROUPS].set(w_group).at[:, :, N_GROUPS:N_GROUPS + N_EXPERTS].set(w_router)
    b_rt = jnp.zeros((n_layers, 1, LANES), F32)
    b_rt = b_rt.at[:, 0, :N_GROUPS].set(b_group).at[:, 0, N_GROUPS:N_GROUPS + N_EXPERTS].set(b_router)
    cos_t, sin_t = _rope_tables(n_lat)
    row3 = lambda a: a[:, None, :]
    conv_b4, b_r4, b_i4 = (a[:, :, None, :] for a in (conv_b, b_rgate, b_igate))
    sink, sg, sb, bo = row3(attn_sink), row3(sgu_ln_g), row3(sgu_ln_b), row3(b_out)
    l1g, l1b, l2g, l2b = row3(ln1_g), row3(ln1_b), row3(ln2_g), row3(ln2_b)

    def layer(l, xc):
        lyr = jnp.full((1,), l, jnp.int32)
        k, v, hf, hb = _kv_lru(lyr, xc, mods, w_ctx, b_ctx, cos_t, sin_t, conv_w, conv_b4,
                               w_ri, b_r4, b_i4, lru_lambda)
        x1, h2, route_i, route_f, counts = _mixer(
            lyr, xc, mods, w_main, b_main, cos_t, sin_t, sink, k, v, hf, hb,
            sg, sb, w_sp, b_sp, wa, wb, wc, wo, bo, l1g, l1b, w_rt, b_rt)
        slot, block_expert, n_used = _slot_tables(route_i, counts[0, :N_EXPERTS], n_blocks)
        buf = _dispatch(slot, h2, n_slots)
        eo = _experts(lyr, block_expert, n_used, buf, w1, w3, w2)
        return _combine(lyr, slot, x1, mods, route_f, l2g, l2b, eo)

    xc = jnp.concatenate([ctx, x], axis=1)
    for l in range(n_layers):
        xc = layer(l, xc)
    return xc[:, CTX_LEN:]
```

```python
import functools

import jax
import jax.numpy as jnp
from jax import lax
from jax.experimental import pallas as pl
from jax.experimental.pallas import tpu as pltpu

F32 = jnp.float32
BF16 = jnp.bfloat16

D_MODEL = 1024
DEPTH = 4
GRID_W = 64
CTX_LEN = 256
HEAD_DIM = 64
ATT_HEADS = 8
KV_HEADS = 2
Q_PER_KV = ATT_HEADS // KV_HEADS
ATT_W = ATT_HEADS * HEAD_DIM
KV_W = KV_HEADS * HEAD_DIM
WINDOW = 128
ROPE_THETA = 10000.0
CHUNK = 128
SGU_GROUPS = 4
SGU_W = SGU_GROUPS * CHUNK
LRU_W = D_MODEL
LRU_BLOCKS = 8
LRU_BW = LRU_W // LRU_BLOCKS
CONV_W = 4
LRU_C = 8.0
N_GROUPS = 4
EXPERTS_PER_GROUP = 8
N_EXPERTS = N_GROUPS * EXPERTS_PER_GROUP
TOP_K = 2
D_EXPERT = 512
N_MOD = 6
ALPHA = (2.0 * DEPTH) ** 0.25
LN_EPS = 1e-6
NEG_INF = -1e30

Q0, U0, V0, G0, Z0 = 0, ATT_W, ATT_W + SGU_W, ATT_W + 2 * SGU_W, ATT_W + 2 * SGU_W + 3 * D_MODEL
MAIN_W = Z0 + LRU_W
CTX_W = 2 * KV_W + LRU_W

TILE = 256
SUBLANES = 8
LANES = 128
EXPERT_BLOCK = 512
GATHER_ROWS = 128
MIB = 1024 * 1024
VMEM_LIMIT = 56 * MIB


def _sigmoid(x):
    return 0.5 * (1.0 + jnp.tanh(0.5 * x))


def _gelu(x):
    return 0.5 * x * (1.0 + jnp.tanh(0.7978845608028654 * (x + 0.044715 * (x * x * x))))


def _layer_norm(x, g, b):
    mu = jnp.mean(x, axis=-1, keepdims=True)
    xc = x - mu
    var = jnp.mean(xc * xc, axis=-1, keepdims=True)
    return xc * lax.rsqrt(var + LN_EPS) * g + b


def _dot(a, b):
    return jnp.dot(a, b, preferred_element_type=F32)


def _dot3(a, b):
    a_hi = a.astype(BF16)
    a_lo = (a - a_hi.astype(F32)).astype(BF16)
    b_hi = b.astype(BF16)
    b_lo = (b - b_hi.astype(F32)).astype(BF16)
    return _dot(a_hi, b_hi) + (_dot(a_hi, b_lo) + _dot(a_lo, b_hi))


def _pack_rows(v):
    n = v.shape[1] // 2
    bits = pltpu.bitcast(v.astype(BF16).astype(F32), jnp.uint32)
    return (bits[:, :n] >> 16) | bits[:, n:]


def _unpack_rows(p):
    lo = pltpu.bitcast(p << 16, F32)
    hi = pltpu.bitcast(p & jnp.uint32(0xFFFF0000), F32)
    return jnp.concatenate([lo, hi], axis=1)


def _rope(x, cos, sin_signed):
    n = x.shape[1] // LANES
    if n > 1:
        cos = jnp.concatenate([cos] * n, axis=1)
        sin_signed = jnp.concatenate([sin_signed] * n, axis=1)
    lane = lax.broadcasted_iota(jnp.int32, x.shape, 1)
    first = (lane & 31) < 16
    w = x.shape[1]
    partner = jnp.where(first, pltpu.roll(x, w - 16, 1), pltpu.roll(x, 16, 1))
    return x * cos + partner * sin_signed


def _mods_kernel(c_ref, w_ref, b_ref, o_ref):
    c = c_ref[...]
    o_ref[...] = _dot3(c * _sigmoid(c), w_ref[...]) + b_ref[...]


def _modulation(cond, w_mod, b_mod):
    n_layers = w_mod.shape[0]
    tn = 1536
    return pl.pallas_call(
        _mods_kernel,
        out_shape=jax.ShapeDtypeStruct((n_layers, SUBLANES, N_MOD * D_MODEL), F32),
        grid=(n_layers, N_MOD * D_MODEL // tn),
        in_specs=[
            pl.BlockSpec((SUBLANES, D_MODEL), lambda l, j: (0, 0)),
            pl.BlockSpec((None, D_MODEL, tn), lambda l, j: (l, 0, j)),
            pl.BlockSpec((None, 1, tn), lambda l, j: (l, 0, j)),
        ],
        out_specs=pl.BlockSpec((None, SUBLANES, tn), lambda l, j: (l, 0, j)),
        compiler_params=pltpu.CompilerParams(dimension_semantics=("arbitrary", "arbitrary"),
                                             vmem_limit_bytes=40 * MIB),
        name="modulation",
    )(cond, w_mod, b_mod.reshape(n_layers, 1, N_MOD * D_MODEL))


def _scan_block(a_ref, u_ref, out_ref, blk, carry, keeps, reverse):
    r0 = pl.multiple_of(blk * SUBLANES, SUBLANES)
    new_carry = []
    for c in range(a_ref.shape[1] // LANES):
        cols = slice(c * LANES, (c + 1) * LANES)
        a = a_ref[pl.ds(r0, SUBLANES), cols]
        u = u_ref[pl.ds(r0, SUBLANES), cols]
        for s, keep in zip((1, 2, 4), keeps):
            shift = SUBLANES - s if reverse else s
            a_sh = jnp.where(keep, pltpu.roll(a, shift, 0), 1.0)
            u_sh = jnp.where(keep, pltpu.roll(u, shift, 0), 0.0)
            u = a * u_sh + u
            a = a * a_sh
        h = u + a * carry[:, cols]
        out_ref[pl.ds(r0, SUBLANES), cols] = h
        new_carry.append(h[0:1, :] if reverse else h[SUBLANES - 1:SUBLANES, :])
    return jnp.concatenate(new_carry, axis=1)


def _scan_tiles(af, uf, cf, hf_ref, ab, ub, cb, hb_ref):
    n_blk = af.shape[0] // SUBLANES
    row = lax.broadcasted_iota(jnp.int32, (SUBLANES, LANES), 0)
    keep_f = [row >= s for s in (1, 2, 4)]
    keep_b = [row < SUBLANES - s for s in (1, 2, 4)]

    def body(j, carry):
        return (_scan_block(af, uf, hf_ref, j, carry[0], keep_f, reverse=False),
                _scan_block(ab, ub, hb_ref, n_blk - 1 - j, carry[1], keep_b, reverse=True))

    cf[...], cb[...] = lax.fori_loop(0, n_blk, body, (cf[...], cb[...]))


def _lru_inputs(t, wri_ref, br_ref, bi_ref, hl, a_ref, u_ref):
    for n in range(LRU_BLOCKS):
        cols = slice(n * LRU_BW, (n + 1) * LRU_BW)
        tn = t[:, cols]
        g = _dot(tn.astype(BF16), wri_ref[n])
        tr = jnp.tanh(0.5 * (g[:, :LRU_BW] + br_ref[:, cols]))
        ti = jnp.tanh(0.5 * (g[:, LRU_BW:] + bi_ref[:, cols]))
        half_log = hl[:, cols]
        a = jnp.exp(half_log + half_log * tr)
        a_ref[:, cols] = a
        u_ref[:, cols] = jnp.sqrt(1.0 - a * a) * ((0.5 * tn) * (1.0 + ti))


def _kv_lru_kernel(ly_ref, xf_ref, xb_ref, mod_ref, wc_ref, bc_ref, cos_ref, sin_ref,
                   cw_ref, cb_ref, wri_ref, br_ref, bi_ref, lam_ref,
                   k_ref, v_ref, hf_ref, hb_ref,
                   xsf, xsb, af, uf, ab, ub, cf, cbk):
    del ly_ref
    i = pl.program_id(1)
    m = mod_ref[...]
    shift, scale = m[0:1, :], 1.0 + m[1:2, :]

    @pl.when(i == 0)
    def _():
        cf[...] = jnp.zeros_like(cf)
        cbk[...] = jnp.zeros_like(cbk)
        xsf[...] = jnp.zeros_like(xsf)
        xsb[...] = jnp.zeros_like(xsb)

    h = (xf_ref[...] * scale + shift).astype(BF16)
    p = _dot(h, wc_ref[...]) + bc_ref[...]
    k_ref[...] = _rope(p[:, :KV_W], cos_ref[...], sin_ref[...]).astype(BF16)
    v_ref[...] = p[:, KV_W:2 * KV_W].astype(BF16)
    xr_f = p[:, 2 * KV_W:]
    h = (xb_ref[...] * scale + shift).astype(BF16)
    xr_b = _dot(h, wc_ref[:, 2 * KV_W:]) + bc_ref[:, 2 * KV_W:]

    inside = jnp.broadcast_to(i > 1, (SUBLANES, LRU_W))
    xsf[0:SUBLANES, :] = jnp.where(inside, xsf[TILE:TILE + SUBLANES, :], 0.0)
    xsb[TILE:TILE + SUBLANES, :] = jnp.where(inside, xsb[0:SUBLANES, :], 0.0)

    xsf[SUBLANES:SUBLANES + TILE, :] = xr_f
    xsb[0:TILE, :] = xr_b

    lam = lam_ref[...]
    sp = (-0.5 * LRU_C) * (jnp.maximum(-lam, 0.0) + jnp.log1p(jnp.exp(-jnp.abs(lam))))

    cw = cw_ref[0]
    t = (cb_ref[0] + cw[0:1, :] * xsf[SUBLANES - 3:SUBLANES - 3 + TILE, :]
         + cw[1:2, :] * xsf[SUBLANES - 2:SUBLANES - 2 + TILE, :]
         + cw[2:3, :] * xsf[SUBLANES - 1:SUBLANES - 1 + TILE, :]
         + cw[3:4, :] * xr_f)
    _lru_inputs(t, wri_ref.at[0], br_ref.at[0], bi_ref.at[0], sp[0:1, :], af, uf)
    cw = cw_ref[1]
    t = (cb_ref[1] + cw[0:1, :] * xr_b
         + cw[1:2, :] * xsb[1:1 + TILE, :]
         + cw[2:3, :] * xsb[2:2 + TILE, :]
         + cw[3:4, :] * xsb[3:3 + TILE, :])
    _lru_inputs(t, wri_ref.at[1], br_ref.at[1], bi_ref.at[1], sp[1:2, :], ab, ub)

    _scan_tiles(af, uf, cf, hf_ref, ab, ub, cbk, hb_ref)


def _kv_lru(lyr, xc, mods, w_ctx, b_ctx, cos_t, sin_t, conv_w, conv_b, w_ri, b_r, b_i, lam):
    bsz, n_tok, _ = xc.shape
    nt = n_tok // TILE
    tile_f = lambda b, i, ly: (b, i, 0)
    tile_b = lambda b, i, ly: (b, jnp.where(i == 0, 0, nt - i), 0)
    lay3 = lambda b, i, ly: (ly[0], 0, 0)
    lay4 = lambda b, i, ly: (ly[0], 0, 0, 0)
    return pl.pallas_call(
        _kv_lru_kernel,
        out_shape=(jax.ShapeDtypeStruct((bsz, n_tok, KV_W), BF16),
                   jax.ShapeDtypeStruct((bsz, n_tok, KV_W), BF16),
                   jax.ShapeDtypeStruct((bsz, n_tok, LRU_W), F32),
                   jax.ShapeDtypeStruct((bsz, n_tok, LRU_W), F32)),
        grid_spec=pltpu.PrefetchScalarGridSpec(
            num_scalar_prefetch=1,
            grid=(bsz, nt),
            in_specs=[
                pl.BlockSpec((None, TILE, D_MODEL), tile_f),
                pl.BlockSpec((None, TILE, D_MODEL), tile_b),
                pl.BlockSpec((None, None, N_MOD, D_MODEL),
                             lambda b, i, ly: (ly[0], jnp.where(i == 0, bsz, b), 0, 0)),
                pl.BlockSpec((None, D_MODEL, CTX_W), lay3),
                pl.BlockSpec((None, 1, CTX_W), lay3),
                pl.BlockSpec((TILE, KV_W), lambda b, i, ly: (i, 0)),
                pl.BlockSpec((TILE, KV_W), lambda b, i, ly: (i, 0)),
                pl.BlockSpec((None, 2, CONV_W, LRU_W), lay4),
                pl.BlockSpec((None, 2, 1, LRU_W), lay4),
                pl.BlockSpec((None, 2, LRU_BLOCKS, LRU_BW, 2 * LRU_BW), lambda b, i, ly: (ly[0], 0, 0, 0, 0)),
                pl.BlockSpec((None, 2, 1, LRU_W), lay4),
                pl.BlockSpec((None, 2, 1, LRU_W), lay4),
                pl.BlockSpec((None, 2, LRU_W), lay3),
            ],
            out_specs=(pl.BlockSpec((None, TILE, KV_W), tile_f),
                       pl.BlockSpec((None, TILE, KV_W), tile_f),
                       pl.BlockSpec((None, TILE, LRU_W), tile_f),
                       pl.BlockSpec((None, TILE, LRU_W), tile_b)),
            scratch_shapes=[pltpu.VMEM((TILE + SUBLANES, LRU_W), F32),
                            pltpu.VMEM((TILE + SUBLANES, LRU_W), F32),
                            pltpu.VMEM((TILE, LRU_W), F32), pltpu.VMEM((TILE, LRU_W), F32),
                            pltpu.VMEM((TILE, LRU_W), F32), pltpu.VMEM((TILE, LRU_W), F32),
                            pltpu.VMEM((1, LRU_W), F32), pltpu.VMEM((1, LRU_W), F32)],
        ),
        compiler_params=pltpu.CompilerParams(dimension_semantics=("arbitrary", "arbitrary"),
                                             vmem_limit_bytes=VMEM_LIMIT),
        name="kv_lru",
    )(lyr, xc, xc, mods, w_ctx, b_ctx, cos_t, sin_t, conv_w, conv_b, w_ri, b_r, b_i, lam)


def _attention(q, k_all, v_all, bias, sink):
    rows = q.shape[0]
    lane = lax.broadcasted_iota(jnp.int32, (rows, LANES), 1)
    low = lane < HEAD_DIM
    parts = []
    for hd in range(ATT_HEADS):
        g = hd // Q_PER_KV
        t = q[:, (hd // 2) * LANES:(hd // 2 + 1) * LANES]
        if hd % 2 != g:
            t = pltpu.roll(t, HEAD_DIM, 1)
        parts.append(jnp.where(low if g == 0 else ~low, t, 0.0))
    qs = jnp.concatenate(parts, axis=0).astype(BF16)
    s = lax.dot_general(qs, k_all, (((1,), (1,)), ((), ())), preferred_element_type=F32)
    s = s + bias
    blk = lax.broadcasted_iota(jnp.int32, (ATT_HEADS * rows, 1), 0)
    sink_col = jnp.zeros((ATT_HEADS * rows, 1), F32)
    for hd in range(ATT_HEADS):
        sink_col = jnp.where((blk >= hd * rows) & (blk < (hd + 1) * rows), sink[:, hd:hd + 1], sink_col)
    mx = jnp.maximum(jnp.max(s, axis=-1, keepdims=True), sink_col)
    p = jnp.exp(s - mx)
    den = jnp.sum(p, axis=-1, keepdims=True) + jnp.exp(sink_col - mx)
    o = _dot(p.astype(BF16), v_all) * (1.0 / den)
    tiles = []
    for t in range(ATT_HEADS // 2):
        g = (2 * t) // Q_PER_KV
        even = o[2 * t * rows:(2 * t + 1) * rows, :]
        odd = o[(2 * t + 1) * rows:(2 * t + 2) * rows, :]
        if g == 1:
            even = pltpu.roll(even, HEAD_DIM, 1)
        else:
            odd = pltpu.roll(odd, HEAD_DIM, 1)
        tiles.append(jnp.where(low, even, odd))
    return jnp.concatenate(tiles, axis=1)


def _mixer_kernel(ly_ref, x_ref, mod_ref, w_ref, b_ref, cos_ref, sin_ref, sink_ref, bias0_ref, bias1_ref,
                  kc_ref, vc_ref, kp_ref, vp_ref, kt_ref, vt_ref, kn_ref, vn_ref,
                  hf_ref, hb_ref, sg_ref, sb_ref, ws_ref, bs_ref,
                  wa_ref, wb_ref, wc_ref, wo_ref, bo_ref, l1g_ref, l1b_ref, wrt_ref, brt_ref,
                  x1_ref, h2_ref, ri_ref, rf_ref, cnt_ref,
                  run_s):
    del ly_ref
    @pl.when((pl.program_id(0) == 0) & (pl.program_id(1) == 0))
    def _():
        run_s[...] = jnp.zeros_like(run_s)

    x = x_ref[...]
    m = mod_ref[...]
    h = (x * (1.0 + m[1:2, :]) + m[0:1, :]).astype(BF16)

    def proj(c0, width):
        return _dot(h, w_ref[:, c0:c0 + width]) + b_ref[:, c0:c0 + width]

    q = _rope(proj(Q0, ATT_W), cos_ref[...], sin_ref[...]) * (HEAD_DIM ** -0.5)
    sink = sink_ref[...]

    kt, vt = kt_ref[...], vt_ref[...]
    att = []
    for half in range(2):
        if half == 0:
            k_all = jnp.concatenate([kc_ref[...], kp_ref[...], kt], axis=0)
            v_all = jnp.concatenate([vc_ref[...], vp_ref[...], vt], axis=0)
            bias = bias0_ref[...]
        else:
            k_all = jnp.concatenate([kc_ref[...], kt, kn_ref[...]], axis=0)
            v_all = jnp.concatenate([vc_ref[...], vt, vn_ref[...]], axis=0)
            bias = bias1_ref[...]
        qh = q[half * WINDOW:(half + 1) * WINDOW, :]
        att.append(_attention(qh, k_all, v_all, bias, sink))
    b_br = jnp.concatenate(att, axis=0)

    u = _gelu(proj(U0, SGU_W))
    vn = _layer_norm(_gelu(proj(V0, SGU_W)), sg_ref[...], sb_ref[...]).astype(BF16)
    mixed = []
    for c in range(TILE // CHUNK):
        row = []
        for g in range(SGU_GROUPS):
            row.append(_dot(ws_ref[g], vn[c * CHUNK:(c + 1) * CHUNK, g * CHUNK:(g + 1) * CHUNK]))
        mixed.append(jnp.concatenate(row, axis=1) + bs_ref[...])
    a_br = u * jnp.concatenate(mixed, axis=0)

    r_br = (hf_ref[...] + hb_ref[...]) * _gelu(proj(Z0, LRU_W))

    y = _sigmoid(proj(G0, D_MODEL)) * _dot(a_br.astype(BF16), wa_ref[...])
    y = y + _sigmoid(proj(G0 + D_MODEL, D_MODEL)) * _dot(b_br.astype(BF16), wb_ref[...])
    y = y + _sigmoid(proj(G0 + 2 * D_MODEL, D_MODEL)) * _dot(r_br.astype(BF16), wc_ref[...])
    out = _dot(y.astype(BF16), wo_ref[...]) + bo_ref[...]
    x1 = _layer_norm(ALPHA * x + m[2:3, :] * out, l1g_ref[...], l1b_ref[...])
    x1_ref[...] = x1
    h2 = x1 * (1.0 + m[4:5, :]) + m[3:4, :]
    h2_ref[...] = _pack_rows(h2)

    logits = _dot3(h2, wrt_ref[...]) + brt_ref[...]
    lane = lax.broadcasted_iota(jnp.int32, (TILE, LANES), 1)
    lane_f = lane.astype(F32)
    big = float(LANES)
    gl = jnp.where(lane < N_GROUPS, logits, NEG_INF)
    gmax = jnp.max(gl, axis=-1, keepdims=True)
    g_idx = jnp.min(jnp.where(gl == gmax, lane_f, big), axis=-1, keepdims=True)
    g_w = 1.0 / jnp.sum(jnp.exp(gl - gmax), axis=-1, keepdims=True)
    lo = N_GROUPS + EXPERTS_PER_GROUP * g_idx
    el = jnp.where((lane_f >= lo) & (lane_f < lo + EXPERTS_PER_GROUP), logits, NEG_INF)
    v1 = jnp.max(el, axis=-1, keepdims=True)
    l1 = jnp.min(jnp.where(el == v1, lane_f, big), axis=-1, keepdims=True)
    el2 = jnp.where(lane_f == l1, NEG_INF, el)
    v2 = jnp.max(el2, axis=-1, keepdims=True)
    l2 = jnp.min(jnp.where(el2 == v2, lane_f, big), axis=-1, keepdims=True)
    e21 = jnp.exp(v2 - v1)
    gate1 = g_w / (1.0 + e21)
    gate2 = g_w * e21 / (1.0 + e21)
    e1 = l1 - N_GROUPS
    e2 = l2 - N_GROUPS

    oh1 =jnp.where(lane_f == e1, 1.0, 0.0)
    oh2 = jnp.where(lane_f == e2, 1.0, 0.0)
    both = oh1 + oh2
    tr = lax.broadcasted_iota(jnp.int32, (TILE, TILE), 0)
    tc = lax.broadcasted_iota(jnp.int32, (TILE, TILE), 1)
    earlier = jnp.where(tc < tr, 1.0, 0.0).astype(BF16)
    before = _dot(earlier, both.astype(BF16)) + run_s[...]
    rank1 = jnp.sum(before * oh1, axis=-1, keepdims=True)
    rank2 = jnp.sum(before * oh2, axis=-1, keepdims=True)
    run_s[...] = run_s[...] + jnp.sum(both, axis=0, keepdims=True)
    cnt_ref[...] = jnp.broadcast_to(run_s[...], cnt_ref.shape)

    ri = jnp.where(lane == 0, e1, jnp.where(lane == 1, e2, jnp.where(lane == 2, rank1,
                                                                      jnp.where(lane == 3, rank2, 0.0))))
    ri_ref[...] = ri.astype(jnp.int32)
    rf_ref[...] = jnp.where(lane == 0, gate1, jnp.where(lane == 1, gate2, 0.0))


def _window_bias():
    r = jnp.arange(ATT_HEADS * WINDOW)[:, None] % WINDOW
    c = jnp.arange(CTX_LEN + 3 * WINDOW)[None, :]
    j_prev = c - CTX_LEN
    j_next = c - (CTX_LEN + 2 * WINDOW)
    in_prev = (j_prev >= 0) & (j_prev < WINDOW)
    in_next = j_next >= 0
    full = ~(in_prev & (j_prev < r)) & ~(in_next & (j_next > r))
    ctx_only = jnp.broadcast_to(c < CTX_LEN, full.shape)
    masks = jnp.stack([jnp.stack([full, full & ~in_prev, ctx_only]),
                       jnp.stack([full, full & ~in_next, ctx_only])])
    return jnp.where(masks, 0.0, NEG_INF).astype(F32)


def _mixer(lyr, xc, mods, w_main, b_main, cos_t, sin_t, sink, k, v, hf, hb,
           sg, sb, w_sp, b_sp, wa, wb, wc, wo, bo, l1g, l1b, w_rt, b_rt):
    bsz, n_tok, _ = xc.shape
    nt = n_tok // TILE
    n_half = n_tok // WINDOW
    once = pl.Buffered(1)
    bias = _window_bias()
    n_q, n_keys = bias.shape[2:]
    variant = lambda i, edge: jnp.where(i == 0, 2, jnp.where(i == edge, 1, 0))
    bias0 = pl.BlockSpec((None, None, n_q, n_keys), lambda b, i, ly: (0, variant(i, 1), 0, 0))
    bias1 = pl.BlockSpec((None, None, n_q, n_keys), lambda b, i, ly: (1, variant(i, nt - 1), 0, 0))
    tile = lambda b, i, ly: (b, i, 0)
    lay3 = lambda b, i, ly: (ly[0], 0, 0)
    flat = lambda b, i, ly: (b * nt + i, 0)
    kv_ctx = pl.BlockSpec((None, CTX_LEN, KV_W), lambda b, i, ly: (b, 0, 0))
    kv_prev = pl.BlockSpec((None, WINDOW, KV_W), lambda b, i, ly: (b, jnp.maximum(2 * i - 1, 0), 0))
    kv_this = pl.BlockSpec((None, TILE, KV_W), tile)
    kv_next = pl.BlockSpec((None, WINDOW, KV_W), lambda b, i, ly: (b, jnp.minimum(2 * i + 2, n_half - 1), 0))
    row_vec = lambda width: pl.BlockSpec((None, 1, width), lay3)
    return pl.pallas_call(
        _mixer_kernel,
        out_shape=(jax.ShapeDtypeStruct((bsz, n_tok, D_MODEL), F32),
                   jax.ShapeDtypeStruct((bsz * n_tok, D_MODEL // 2), jnp.uint32),
                   jax.ShapeDtypeStruct((bsz * n_tok, LANES), jnp.int32),
                   jax.ShapeDtypeStruct((bsz * n_tok, LANES), F32),
                   jax.ShapeDtypeStruct((SUBLANES, LANES), F32)),
        grid_spec=pltpu.PrefetchScalarGridSpec(
            num_scalar_prefetch=1,
            grid=(bsz, nt),
            in_specs=[
                pl.BlockSpec((None, TILE, D_MODEL), tile),
                pl.BlockSpec((None, None, N_MOD, D_MODEL),
                             lambda b, i, ly: (ly[0], jnp.where(i == 0, bsz, b), 0, 0)),
                pl.BlockSpec((None, D_MODEL, MAIN_W), lay3, pipeline_mode=once),
                row_vec(MAIN_W),
                pl.BlockSpec((TILE, KV_W), lambda b, i, ly: (i, 0)),
                pl.BlockSpec((TILE, KV_W), lambda b, i, ly: (i, 0)),
                pl.BlockSpec((None, 1, ATT_HEADS), lay3),
                bias0, bias1,
                kv_ctx, kv_ctx, kv_prev, kv_prev, kv_this, kv_this, kv_next, kv_next,
                pl.BlockSpec((None, TILE, LRU_W), tile),
                pl.BlockSpec((None, TILE, LRU_W), tile),
                row_vec(SGU_W), row_vec(SGU_W),
                pl.BlockSpec((None, SGU_GROUPS, CHUNK, CHUNK), lambda b, i, ly: (ly[0], 0, 0, 0)),
                pl.BlockSpec((None, CHUNK, SGU_W), lay3),
                pl.BlockSpec((None, SGU_W, D_MODEL), lay3, pipeline_mode=once),
                pl.BlockSpec((None, ATT_W, D_MODEL), lay3, pipeline_mode=once),
                pl.BlockSpec((None, LRU_W, D_MODEL), lay3, pipeline_mode=once),
                pl.BlockSpec((None, D_MODEL, D_MODEL), lay3, pipeline_mode=once),
                row_vec(D_MODEL), row_vec(D_MODEL), row_vec(D_MODEL),
                pl.BlockSpec((None, D_MODEL, LANES), lay3),
                row_vec(LANES),
            ],
            out_specs=(pl.BlockSpec((None, TILE, D_MODEL), tile),
                       pl.BlockSpec((TILE, D_MODEL // 2), flat),
                       pl.BlockSpec((TILE, LANES), flat),
                       pl.BlockSpec((TILE, LANES), flat),
                       pl.BlockSpec((SUBLANES, LANES), lambda b, i, ly: (0, 0))),
            scratch_shapes=[pltpu.VMEM((1, LANES), F32)],
        ),
        compiler_params=pltpu.CompilerParams(dimension_semantics=("arbitrary", "arbitrary"),
                                             vmem_limit_bytes=VMEM_LIMIT),
        name="mixer",
    )(lyr, xc, mods, w_main, b_main, cos_t, sin_t, sink, bias, bias, k, v, k, v, k, v, k, v, hf, hb,
      sg, sb, w_sp, b_sp, wa, wb, wc, wo, bo, l1g, l1b, w_rt, b_rt)


def _dispatch_kernel(slot_ref, h2_ref, buf_in_ref, buf_ref, sem):
    del buf_in_ref
    base = pl.program_id(0) * TILE

    copies = [(k, pltpu.make_async_copy(h2_ref.at[pl.ds(t, 1)],
                                        buf_ref.at[pl.ds(slot_ref[TOP_K * (base + t) + k], 1)], sem))
              for t in range(TILE) for k in range(TOP_K)]
    for k, cp in copies:
        cp.start(priority=k)
    for _, cp in copies:
        cp.wait()


def _dispatch(slot, h2, n_slots):
    n_tok, width = h2.shape
    return pl.pallas_call(
        _dispatch_kernel,
        out_shape=jax.ShapeDtypeStruct((n_slots, width), h2.dtype),
        grid_spec=pltpu.PrefetchScalarGridSpec(
            num_scalar_prefetch=1,
            grid=(n_tok // TILE,),
            in_specs=[pl.BlockSpec((TILE, width), lambda i, sl: (i, 0)),
                      pl.BlockSpec(memory_space=pl.ANY)],
            out_specs=pl.BlockSpec(memory_space=pl.ANY),
            scratch_shapes=[pltpu.SemaphoreType.DMA(())],
        ),
        input_output_aliases={2: 0},
        compiler_params=pltpu.CompilerParams(dimension_semantics=("arbitrary",)),
        name="dispatch",
    )(slot, h2, jnp.zeros((n_slots, width), h2.dtype))


def _experts_kernel(ly_ref, be_ref, nu_ref, x_ref, w1_ref, w3_ref, w2_ref, o_ref, w1s, w3s, w2s):
    del ly_ref
    i = pl.program_id(0)
    changed = (i == 0) | (be_ref[i] != be_ref[jnp.maximum(i - 1, 0)])

    @pl.when((i < nu_ref[0]) & changed)
    def _():
        w1s[...] = w1_ref[...].astype(BF16)
        w3s[...] = w3_ref[...].astype(BF16)
        w2s[...] = w2_ref[...].astype(BF16)

    @pl.when(i < nu_ref[0])
    def _():
        x = _unpack_rows(x_ref[...]).astype(BF16)
        h1 = _dot(x, w1s[...])
        h3 = _dot(x, w3s[...])
        act = (h1 * _sigmoid(h1) * h3).astype(BF16)
        o_ref[...] = _pack_rows(_dot(act, w2s[...]))

    @pl.when(i >= nu_ref[0])
    def _():
        o_ref[...] = jnp.zeros_like(o_ref)


def _experts(lyr, block_expert, n_used, buf, w1, w3, w2):
    n_blocks = buf.shape[0] // EXPERT_BLOCK

    def blk(i, ly, be, nu):
        return (jnp.minimum(i, nu[0] - 1), 0)

    def wsel(i, ly, be, nu):
        return (ly[0], be[jnp.minimum(i, nu[0] - 1)], 0, 0)

    return pl.pallas_call(
        _experts_kernel,
        out_shape=jax.ShapeDtypeStruct(buf.shape, buf.dtype),
        grid_spec=pltpu.PrefetchScalarGridSpec(
            num_scalar_prefetch=3,
            grid=(n_blocks,),
            in_specs=[pl.BlockSpec((EXPERT_BLOCK, D_MODEL // 2), blk),
                      pl.BlockSpec((None, None, D_MODEL, D_EXPERT), wsel),
                      pl.BlockSpec((None, None, D_MODEL, D_EXPERT), wsel),
                      pl.BlockSpec((None, None, D_EXPERT, D_MODEL), wsel)],
            out_specs=pl.BlockSpec((EXPERT_BLOCK, D_MODEL // 2), lambda i, ly, be, nu: (i, 0)),
            scratch_shapes=[pltpu.VMEM((D_MODEL, D_EXPERT), BF16),
                            pltpu.VMEM((D_MODEL, D_EXPERT), BF16),
                            pltpu.VMEM((D_EXPERT, D_MODEL), BF16)],
        ),
        compiler_params=pltpu.CompilerParams(dimension_semantics=("arbitrary",),
                                             vmem_limit_bytes=40 * MIB),
        name="experts",
    )(lyr, block_expert, n_used, buf, w1, w3, w2)


def _combine_kernel(ly_ref, slot_ref, x1_ref, mod_ref, rf_ref, g_ref, b_ref, eo_ref, o_ref, gbuf, sems,
                    *, n_steps):
    del ly_ref
    s = pl.program_id(0)
    par = s % 2
    n_groups = TILE // GATHER_ROWS
    gate2 = mod_ref[5:6, :]
    ln_g, ln_b = g_ref[...], b_ref[...]

    def group_copies(tile, buf, j):
        copies = []
        for r in range(GATHER_ROWS):
            t = j * GATHER_ROWS + r
            for k in range(TOP_K):
                src = eo_ref.at[pl.ds(slot_ref[TOP_K * (tile * TILE + t) + k], 1)]
                copies.append((k, pltpu.make_async_copy(src, gbuf.at[buf, k, pl.ds(t, 1)], sems.at[buf, j])))
        return copies

    def start_group(tile, buf, j):
        for k, cp in group_copies(tile, buf, j):
            cp.start(priority=k)

    def wait_group(tile, buf, j):
        for _, cp in group_copies(tile, buf, j):
            cp.wait()

    @pl.when(s == 0)
    def _():
        for j in range(n_groups):
            start_group(s, par, j)

    nxt = jnp.minimum(s + 1, n_steps - 1)

    for j in range(n_groups):
        start_group(nxt, 1 - par, j)
        wait_group(s, par, j)
        rows = slice(j * GATHER_ROWS, (j + 1) * GATHER_ROWS)
        rf = rf_ref[rows, :]
        y = (rf[:, 0:1] * _unpack_rows(gbuf[par, 0, rows, :])
             + rf[:, 1:2] * _unpack_rows(gbuf[par, 1, rows, :]))
        o_ref[rows, :] = _layer_norm(ALPHA * x1_ref[rows, :] + gate2 * y, ln_g, ln_b)

    @pl.when(s == n_steps - 1)
    def _():
        for j in range(n_groups):
            wait_group(nxt, 1 - par, j)


def _combine(lyr, slot, x1, mods, rf, l2g, l2b, eo):
    bsz, n_tok, _ = x1.shape
    nt = n_tok // TILE
    n_steps = bsz * nt
    tile = lambda s, ly, sl: (s // nt, s % nt, 0)
    lay3 = lambda s, ly, sl: (ly[0], 0, 0)
    return pl.pallas_call(
        functools.partial(_combine_kernel, n_steps=n_steps),
        out_shape=jax.ShapeDtypeStruct((bsz, n_tok, D_MODEL), F32),
        grid_spec=pltpu.PrefetchScalarGridSpec(
            num_scalar_prefetch=2,
            grid=(n_steps,),
            in_specs=[pl.BlockSpec((None, TILE, D_MODEL), tile),
                      pl.BlockSpec((None, None, N_MOD, D_MODEL),
                                   lambda s, ly, sl: (ly[0], jnp.where(s % nt == 0, bsz, s // nt), 0, 0)),
                      pl.BlockSpec((TILE, LANES), lambda s, ly, sl: (s, 0)),
                      pl.BlockSpec((None, 1, D_MODEL), lay3),
                      pl.BlockSpec((None, 1, D_MODEL), lay3),
                      pl.BlockSpec(memory_space=pl.ANY)],
            out_specs=pl.BlockSpec((None, TILE, D_MODEL), tile),
            scratch_shapes=[pltpu.VMEM((2, TOP_K, TILE, D_MODEL // 2), jnp.uint32),
                            pltpu.SemaphoreType.DMA((2, TILE // GATHER_ROWS))],
        ),
        compiler_params=pltpu.CompilerParams(dimension_semantics=("arbitrary",),
                                             vmem_limit_bytes=40 * MIB),
        name="combine",
    )(lyr, slot, x1, mods, rf, l2g, l2b, eo)


def _slot_tables(route_i, counts, n_blocks):
    counts = counts.astype(jnp.int32)
    padded = (counts + EXPERT_BLOCK - 1) // EXPERT_BLOCK * EXPERT_BLOCK
    pad_ends = jnp.cumsum(padded)
    pad_starts = pad_ends - padded
    expert = route_i[:, :TOP_K]
    rank = route_i[:, TOP_K:2 * TOP_K]
    is_e = expert[..., None] == jnp.arange(N_EXPERTS, dtype=jnp.int32)
    slot = (jnp.sum(jnp.where(is_e, pad_starts, 0), axis=-1) + rank).reshape(-1)
    block_row = jnp.arange(n_blocks, dtype=jnp.int32) * EXPERT_BLOCK
    block_expert = jnp.minimum(jnp.sum((pad_ends[None, :] <= block_row[:, None]).astype(jnp.int32), axis=1),
                               N_EXPERTS - 1)
    n_used = (pad_ends[-1:] // EXPERT_BLOCK).astype(jnp.int32)
    return slot, block_expert, n_used


def _rope_tables(n_lat):
    rows = n_lat // GRID_W
    row = jnp.repeat(jnp.arange(rows, dtype=F32), GRID_W)
    col = jnp.tile(jnp.arange(GRID_W, dtype=F32), rows)
    axis_dim = HEAD_DIM // 2
    inv_freq = ROPE_THETA ** (-jnp.arange(0, axis_dim, 2, dtype=F32) / axis_dim)
    ang_r = row[:, None] * inv_freq
    ang_c = col[:, None] * inv_freq
    cos = jnp.concatenate([jnp.cos(ang_r)] * 2 + [jnp.cos(ang_c)] * 2, axis=1)
    sin = jnp.concatenate([-jnp.sin(ang_r), jnp.sin(ang_r), -jnp.sin(ang_c), jnp.sin(ang_c)], axis=1)
    cos = jnp.concatenate([jnp.ones((CTX_LEN, HEAD_DIM), F32), cos], axis=0)
    sin = jnp.concatenate([jnp.zeros((CTX_LEN, HEAD_DIM), F32), sin], axis=0)
    return jnp.tile(cos, (1, KV_HEADS)), jnp.tile(sin, (1, KV_HEADS))


def kernel(x, c, ctx, c_ctx, w_mod, b_mod, w_in, b_in, sgu_ln_g, sgu_ln_b, w_spatial, b_spatial, attn_sink, conv_w, conv_b, w_rgate, b_rgate, w_igate, b_igate, lru_lambda, w_proj_a, w_proj_b, w_proj_c, w_out, b_out, ln1_g, ln1_b, ln2_g, ln2_b, w_group, b_group, w_router, b_router, w1, w3, w2):
    bsz, n_lat, _ = x.shape
    n_layers = w_mod.shape[0]
    assert ctx.shape[1] == CTX_LEN == TILE and n_lat % TILE == 0 and bsz < SUBLANES
    n_tok = CTX_LEN + n_lat
    n_blocks = -(-bsz * n_tok * TOP_K // EXPERT_BLOCK) + N_EXPERTS
    n_slots = n_blocks * EXPERT_BLOCK

    cond = jnp.zeros((SUBLANES, D_MODEL), F32).at[:bsz].set(c).at[bsz].set(c_ctx)
    mods = _modulation(cond, w_mod, b_mod).reshape(n_layers, SUBLANES, N_MOD, D_MODEL)
    w_main = w_in[:, :, :MAIN_W].astype(BF16)
    w_ctx = w_in[:, :, MAIN_W:].astype(BF16)
    b_main = b_in[:, None, :MAIN_W]
    b_ctx = b_in[:, None, MAIN_W:]
    w_ri = jnp.concatenate([w_rgate, w_igate], axis=-1).astype(BF16)
    w_sp = w_spatial.astype(BF16)
    b_sp = jnp.repeat(jnp.swapaxes(b_spatial, 1, 2), CHUNK, axis=2)
    wa, wb, wc, wo = (w.astype(BF16) for w in (w_proj_a, w_proj_b, w_proj_c, w_out))
    w_rt = jnp.zeros((n_layers, D_MODEL, LANES), F32)
    w_rt = w_rt.at[:, :, :N_GROUPS].set(w_group).at[:, :, N_GROUPS:N_GROUPS + N_EXPERTS].set(w_router)
    b_rt = jnp.zeros((n_layers, 1, LANES), F32)
    b_rt = b_rt.at[:, 0, :N_GROUPS].set(b_group).at[:, 0, N_GROUPS:N_GROUPS + N_EXPERTS].set(b_router)
    cos_t, sin_t = _rope_tables(n_lat)
    row3 = lambda a: a[:, None, :]
    conv_b4, b_r4, b_i4 = (a[:, :, None, :] for a in (conv_b, b_rgate, b_igate))
    sink, sg, sb, bo = row3(attn_sink), row3(sgu_ln_g), row3(sgu_ln_b), row3(b_out)
    l1g, l1b, l2g, l2b = row3(ln1_g), row3(ln1_b), row3(ln2_g), row3(ln2_b)

    def layer(l, xc):
        lyr = jnp.full((1,), l, jnp.int32)
        k, v, hf, hb = _kv_lru(lyr, xc, mods, w_ctx, b_ctx, cos_t, sin_t, conv_w, conv_b4,
                               w_ri, b_r4, b_i4, lru_lambda)
        x1, h2, route_i, route_f, counts = _mixer(
            lyr, xc, mods, w_main, b_main, cos_t, sin_t, sink, k, v, hf, hb,
            sg, sb, w_sp, b_sp, wa, wb, wc, wo, bo, l1g, l1b, w_rt, b_rt)
        slot, block_expert, n_used = _slot_tables(route_i, counts[0, :N_EXPERTS], n_blocks)
        buf = _dispatch(slot, h2, n_slots)
        eo = _experts(lyr, block_expert, n_used, buf, w1, w3, w2)
        return _combine(lyr, slot, x1, mods, route_f, l2g, l2b, eo)

    xc = jnp.concatenate([ctx, x], axis=1)
    for l in range(n_layers):
        xc = layer(l, xc)
    return xc[:, CTX_LEN:]
```

```python
import functools

import jax
import jax.numpy as jnp
from jax import lax
from jax.experimental import pallas as pl
from jax.experimental.pallas import tpu as pltpu

F32 = jnp.float32
BF16 = jnp.bfloat16

D_MODEL = 1024
DEPTH = 4
GRID_W = 64
CTX_LEN = 256
HEAD_DIM = 64
ATT_HEADS = 8
KV_HEADS = 2
Q_PER_KV = ATT_HEADS // KV_HEADS
ATT_W = ATT_HEADS * HEAD_DIM
KV_W = KV_HEADS * HEAD_DIM
WINDOW = 128
ROPE_THETA = 10000.0
CHUNK = 128
SGU_GROUPS = 4
SGU_W = SGU_GROUPS * CHUNK
LRU_W = D_MODEL
LRU_BLOCKS = 8
LRU_BW = LRU_W // LRU_BLOCKS
CONV_W = 4
LRU_C = 8.0
N_GROUPS = 4
EXPERTS_PER_GROUP = 8
N_EXPERTS = N_GROUPS * EXPERTS_PER_GROUP
TOP_K = 2
D_EXPERT = 512
N_MOD = 6
ALPHA = (2.0 * DEPTH) ** 0.25
LN_EPS = 1e-6
NEG_INF = -1e30

Q0, U0, V0, G0, Z0 = 0, ATT_W, ATT_W + SGU_W, ATT_W + 2 * SGU_W, ATT_W + 2 * SGU_W + 3 * D_MODEL
MAIN_W = Z0 + LRU_W
CTX_W = 2 * KV_W + LRU_W

TILE = 256
SUBLANES = 8
LANES = 128
EXPERT_BLOCK = 512
GATHER_ROWS = 128
MIB = 1024 * 1024
VMEM_LIMIT = 56 * MIB


def _sigmoid(x):
    return 0.5 * (1.0 + jnp.tanh(0.5 * x))


def _gelu(x):
    return 0.5 * x * (1.0 + jnp.tanh(0.7978845608028654 * (x + 0.044715 * (x * x * x))))


def _layer_norm(x, g, b):
    mu = jnp.mean(x, axis=-1, keepdims=True)
    xc = x - mu
    var = jnp.mean(xc * xc, axis=-1, keepdims=True)
    return xc * lax.rsqrt(var + LN_EPS) * g + b


def _dot(a, b):
    return jnp.dot(a, b, preferred_element_type=F32)


def _dot3(a, b):
    a_hi = a.astype(BF16)
    a_lo = (a - a_hi.astype(F32)).astype(BF16)
    b_hi = b.astype(BF16)
    b_lo = (b - b_hi.astype(F32)).astype(BF16)
    return _dot(a_hi, b_hi) + (_dot(a_hi, b_lo) + _dot(a_lo, b_hi))


def _pack_rows(v):
    n = v.shape[1] // 2
    bits = pltpu.bitcast(v.astype(BF16).astype(F32), jnp.uint32)
    return (bits[:, :n] >> 16) | bits[:, n:]


def _unpack_rows(p):
    lo = pltpu.bitcast(p << 16, F32)
    hi = pltpu.bitcast(p & jnp.uint32(0xFFFF0000), F32)
    return jnp.concatenate([lo, hi], axis=1)


def _rope(x, cos, sin_signed):
    n = x.shape[1] // LANES
    if n > 1:
        cos = jnp.concatenate([cos] * n, axis=1)
        sin_signed = jnp.concatenate([sin_signed] * n, axis=1)
    lane = lax.broadcasted_iota(jnp.int32, x.shape, 1)
    first = (lane & 31) < 16
    w = x.shape[1]
    partner = jnp.where(first, pltpu.roll(x, w - 16, 1), pltpu.roll(x, 16, 1))
    return x * cos + partner * sin_signed


def _mods_kernel(c_ref, w_ref, b_ref, o_ref):
    c = c_ref[...]
    o_ref[...] = _dot3(c * _sigmoid(c), w_ref[...]) + b_ref[...]


def _modulation(cond, w_mod, b_mod):
    n_layers = w_mod.shape[0]
    tn = 1536
    return pl.pallas_call(
        _mods_kernel,
        out_shape=jax.ShapeDtypeStruct((n_layers, SUBLANES, N_MOD * D_MODEL), F32),
        grid=(n_layers, N_MOD * D_MODEL // tn),
        in_specs=[
            pl.BlockSpec((SUBLANES, D_MODEL), lambda l, j: (0, 0)),
            pl.BlockSpec((None, D_MODEL, tn), lambda l, j: (l, 0, j)),
            pl.BlockSpec((None, 1, tn), lambda l, j: (l, 0, j)),
        ],
        out_specs=pl.BlockSpec((None, SUBLANES, tn), lambda l, j: (l, 0, j)),
        compiler_params=pltpu.CompilerParams(dimension_semantics=("arbitrary", "arbitrary"),
                                             vmem_limit_bytes=40 * MIB),
        name="modulation",
    )(cond, w_mod, b_mod.reshape(n_layers, 1, N_MOD * D_MODEL))


def _scan_block(a_ref, u_ref, out_ref, blk, carry, keeps, reverse):
    r0 = pl.multiple_of(blk * SUBLANES, SUBLANES)
    new_carry = []
    for c in range(a_ref.shape[1] // LANES):
        cols = slice(c * LANES, (c + 1) * LANES)
        a = a_ref[pl.ds(r0, SUBLANES), cols]
        u = u_ref[pl.ds(r0, SUBLANES), cols]
        for s, keep in zip((1, 2, 4), keeps):
            shift = SUBLANES - s if reverse else s
            a_sh = jnp.where(keep, pltpu.roll(a, shift, 0), 1.0)
            u_sh = jnp.where(keep, pltpu.roll(u, shift, 0), 0.0)
            u = a * u_sh + u
            a = a * a_sh
        h = u + a * carry[:, cols]
        out_ref[pl.ds(r0, SUBLANES), cols] = h
        new_carry.append(h[0:1, :] if reverse else h[SUBLANES - 1:SUBLANES, :])
    return jnp.concatenate(new_carry, axis=1)


def _scan_tiles(af, uf, cf, hf_ref, ab, ub, cb, hb_ref):
    n_blk = af.shape[0] // SUBLANES
    row = lax.broadcasted_iota(jnp.int32, (SUBLANES, LANES), 0)
    keep_f = [row >= s for s in (1, 2, 4)]
    keep_b = [row < SUBLANES - s for s in (1, 2, 4)]

    def body(j, carry):
        return (_scan_block(af, uf, hf_ref, j, carry[0], keep_f, reverse=False),
                _scan_block(ab, ub, hb_ref, n_blk - 1 - j, carry[1], keep_b, reverse=True))

    cf[...], cb[...] = lax.fori_loop(0, n_blk, body, (cf[...], cb[...]))


def _lru_inputs(t, wri_ref, br_ref, bi_ref, hl, a_ref, u_ref):
    for n in range(LRU_BLOCKS):
        cols = slice(n * LRU_BW, (n + 1) * LRU_BW)
        tn = t[:, cols]
        g = _dot(tn.astype(BF16), wri_ref[n])
        tr = jnp.tanh(0.5 * (g[:, :LRU_BW] + br_ref[:, cols]))
        ti = jnp.tanh(0.5 * (g[:, LRU_BW:] + bi_ref[:, cols]))
        half_log = hl[:, cols]
        a = jnp.exp(half_log + half_log * tr)
        a_ref[:, cols] = a
        u_ref[:, cols] = jnp.sqrt(1.0 - a * a) * ((0.5 * tn) * (1.0 + ti))


def _kv_lru_kernel(ly_ref, xf_ref, xb_ref, mod_ref, wc_ref, bc_ref, cos_ref, sin_ref,
                   cw_ref, cb_ref, wri_ref, br_ref, bi_ref, lam_ref,
                   k_ref, v_ref, hf_ref, hb_ref,
                   xsf, xsb, af, uf, ab, ub, cf, cbk):
    del ly_ref
    i = pl.program_id(1)
    m = mod_ref[...]
    shift, scale = m[0:1, :], 1.0 + m[1:2, :]

    @pl.when(i == 0)
    def _():
        cf[...] = jnp.zeros_like(cf)
        cbk[...] = jnp.zeros_like(cbk)
        xsf[...] = jnp.zeros_like(xsf)
        xsb[...] = jnp.zeros_like(xsb)

    h = (xf_ref[...] * scale + shift).astype(BF16)
    p = _dot(h, wc_ref[...]) + bc_ref[...]
    k_ref[...] = _rope(p[:, :KV_W], cos_ref[...], sin_ref[...]).astype(BF16)
    v_ref[...] = p[:, KV_W:2 * KV_W].astype(BF16)
    xr_f = p[:, 2 * KV_W:]
    h = (xb_ref[...] * scale + shift).astype(BF16)
    xr_b = _dot(h, wc_ref[:, 2 * KV_W:]) + bc_ref[:, 2 * KV_W:]

    inside = jnp.broadcast_to(i > 1, (SUBLANES, LRU_W))
    xsf[0:SUBLANES, :] = jnp.where(inside, xsf[TILE:TILE + SUBLANES, :], 0.0)
    xsb[TILE:TILE + SUBLANES, :] = jnp.where(inside, xsb[0:SUBLANES, :], 0.0)

    xsf[SUBLANES:SUBLANES + TILE, :] = xr_f
    xsb[0:TILE, :] = xr_b

    lam = lam_ref[...]
    sp = (-0.5 * LRU_C) * (jnp.maximum(-lam, 0.0) + jnp.log1p(jnp.exp(-jnp.abs(lam))))

    cw = cw_ref[0]
    t = (cb_ref[0] + cw[0:1, :] * xsf[SUBLANES - 3:SUBLANES - 3 + TILE, :]
         + cw[1:2, :] * xsf[SUBLANES - 2:SUBLANES - 2 + TILE, :]
         + cw[2:3, :] * xsf[SUBLANES - 1:SUBLANES - 1 + TILE, :]
         + cw[3:4, :] * xr_f)
    _lru_inputs(t, wri_ref.at[0], br_ref.at[0], bi_ref.at[0], sp[0:1, :], af, uf)
    cw = cw_ref[1]
    t = (cb_ref[1] + cw[0:1, :] * xr_b
         + cw[1:2, :] * xsb[1:1 + TILE, :]
         + cw[2:3, :] * xsb[2:2 + TILE, :]
         + cw[3:4, :] * xsb[3:3 + TILE, :])
    _lru_inputs(t, wri_ref.at[1], br_ref.at[1], bi_ref.at[1], sp[1:2, :], ab, ub)

    _scan_tiles(af, uf, cf, hf_ref, ab, ub, cbk, hb_ref)


def _kv_lru(lyr, xc, mods, w_ctx, b_ctx, cos_t, sin_t, conv_w, conv_b, w_ri, b_r, b_i, lam):
    bsz, n_tok, _ = xc.shape
    nt = n_tok // TILE
    tile_f = lambda b, i, ly: (b, i, 0)
    tile_b = lambda b, i, ly: (b, jnp.where(i == 0, 0, nt - i), 0)
    lay3 = lambda b, i, ly: (ly[0], 0, 0)
    lay4 = lambda b, i, ly: (ly[0], 0, 0, 0)
    return pl.pallas_call(
        _kv_lru_kernel,
        out_shape=(jax.ShapeDtypeStruct((bsz, n_tok, KV_W), BF16),
                   jax.ShapeDtypeStruct((bsz, n_tok, KV_W), BF16),
                   jax.ShapeDtypeStruct((bsz, n_tok, LRU_W), F32),
                   jax.ShapeDtypeStruct((bsz, n_tok, LRU_W), F32)),
        grid_spec=pltpu.PrefetchScalarGridSpec(
            num_scalar_prefetch=1,
            grid=(bsz, nt),
            in_specs=[
                pl.BlockSpec((None, TILE, D_MODEL), tile_f),
                pl.BlockSpec((None, TILE, D_MODEL), tile_b),
                pl.BlockSpec((None, None, N_MOD, D_MODEL),
                             lambda b, i, ly: (ly[0], jnp.where(i == 0, bsz, b), 0, 0)),
                pl.BlockSpec((None, D_MODEL, CTX_W), lay3),
                pl.BlockSpec((None, 1, CTX_W), lay3),
                pl.BlockSpec((TILE, KV_W), lambda b, i, ly: (i, 0)),
                pl.BlockSpec((TILE, KV_W), lambda b, i, ly: (i, 0)),
                pl.BlockSpec((None, 2, CONV_W, LRU_W), lay4),
                pl.BlockSpec((None, 2, 1, LRU_W), lay4),
                pl.BlockSpec((None, 2, LRU_BLOCKS, LRU_BW, 2 * LRU_BW), lambda b, i, ly: (ly[0], 0, 0, 0, 0)),
                pl.BlockSpec((None, 2, 1, LRU_W), lay4),
                pl.BlockSpec((None, 2, 1, LRU_W), lay4),
                pl.BlockSpec((None, 2, LRU_W), lay3),
            ],
            out_specs=(pl.BlockSpec((None, TILE, KV_W), tile_f),
                       pl.BlockSpec((None, TILE, KV_W), tile_f),
                       pl.BlockSpec((None, TILE, LRU_W), tile_f),
                       pl.BlockSpec((None, TILE, LRU_W), tile_b)),
            scratch_shapes=[pltpu.VMEM((TILE + SUBLANES, LRU_W), F32),
                            pltpu.VMEM((TILE + SUBLANES, LRU_W), F32),
                            pltpu.VMEM((TILE, LRU_W), F32), pltpu.VMEM((TILE, LRU_W), F32),
                            pltpu.VMEM((TILE, LRU_W), F32), pltpu.VMEM((TILE, LRU_W), F32),
                            pltpu.VMEM((1, LRU_W), F32), pltpu.VMEM((1, LRU_W), F32)],
        ),
        compiler_params=pltpu.CompilerParams(dimension_semantics=("arbitrary", "arbitrary"),
                                             vmem_limit_bytes=VMEM_LIMIT),
        name="kv_lru",
    )(lyr, xc, xc, mods, w_ctx, b_ctx, cos_t, sin_t, conv_w, conv_b, w_ri, b_r, b_i, lam)


def _attention(q, k_all, v_all, bias, sink):
    rows = q.shape[0]
    lane = lax.broadcasted_iota(jnp.int32, (rows, LANES), 1)
    low = lane < HEAD_DIM
    parts = []
    for hd in range(ATT_HEADS):
        g = hd // Q_PER_KV
        t = q[:, (hd // 2) * LANES:(hd // 2 + 1) * LANES]
        if hd % 2 != g:
            t = pltpu.roll(t, HEAD_DIM, 1)
        parts.append(jnp.where(low if g == 0 else ~low, t, 0.0))
    qs = jnp.concatenate(parts, axis=0).astype(BF16)
    s = lax.dot_general(qs, k_all, (((1,), (1,)), ((), ())), preferred_element_type=F32)
    s = s + bias
    blk = lax.broadcasted_iota(jnp.int32, (ATT_HEADS * rows, 1), 0)
    sink_col = jnp.zeros((ATT_HEADS * rows, 1), F32)
    for hd in range(ATT_HEADS):
        sink_col = jnp.where((blk >= hd * rows) & (blk < (hd + 1) * rows), sink[:, hd:hd + 1], sink_col)
    mx = jnp.maximum(jnp.max(s, axis=-1, keepdims=True), sink_col)
    p = jnp.exp(s - mx)
    den = jnp.sum(p, axis=-1, keepdims=True) + jnp.exp(sink_col - mx)
    o = _dot(p.astype(BF16), v_all) * (1.0 / den)
    tiles = []
    for t in range(ATT_HEADS // 2):
        g = (2 * t) // Q_PER_KV
        even = o[2 * t * rows:(2 * t + 1) * rows, :]
        odd = o[(2 * t + 1) * rows:(2 * t + 2) * rows, :]
        if g == 1:
            even = pltpu.roll(even, HEAD_DIM, 1)
        else:
            odd = pltpu.roll(odd, HEAD_DIM, 1)
        tiles.append(jnp.where(low, even, odd))
    return jnp.concatenate(tiles, axis=1)


def _mixer_kernel(ly_ref, x_ref, mod_ref, w_ref, b_ref, cos_ref, sin_ref, sink_ref, bias0_ref, bias1_ref,
                  kc_ref, vc_ref, kp_ref, vp_ref, kt_ref, vt_ref, kn_ref, vn_ref,
                  hf_ref, hb_ref, sg_ref, sb_ref, ws_ref, bs_ref,
                  wa_ref, wb_ref, wc_ref, wo_ref, bo_ref, l1g_ref, l1b_ref, wrt_ref, brt_ref,
                  x1_ref, h2_ref, ri_ref, rf_ref, cnt_ref,
                  run_s):
    del ly_ref
    @pl.when((pl.program_id(0) == 0) & (pl.program_id(1) == 0))
    def _():
        run_s[...] = jnp.zeros_like(run_s)

    x = x_ref[...]
    m = mod_ref[...]
    h = (x * (1.0 + m[1:2, :]) + m[0:1, :]).astype(BF16)

    def proj(c0, width):
        return _dot(h, w_ref[:, c0:c0 + width]) + b_ref[:, c0:c0 + width]

    q = _rope(proj(Q0, ATT_W), cos_ref[...], sin_ref[...]) * (HEAD_DIM ** -0.5)
    sink = sink_ref[...]

    kt, vt = kt_ref[...], vt_ref[...]
    att = []
    for half in range(2):
        if half == 0:
            k_all = jnp.concatenate([kc_ref[...], kp_ref[...], kt], axis=0)
            v_all = jnp.concatenate([vc_ref[...], vp_ref[...], vt], axis=0)
            bias = bias0_ref[...]
        else:
            k_all = jnp.concatenate([kc_ref[...], kt, kn_ref[...]], axis=0)
            v_all = jnp.concatenate([vc_ref[...], vt, vn_ref[...]], axis=0)
            bias = bias1_ref[...]
        qh = q[half * WINDOW:(half + 1) * WINDOW, :]
        att.append(_attention(qh, k_all, v_all, bias, sink))
    b_br = jnp.concatenate(att, axis=0)

    u = _gelu(proj(U0, SGU_W))
    vn = _layer_norm(_gelu(proj(V0, SGU_W)), sg_ref[...], sb_ref[...]).astype(BF16)
    mixed = []
    for c in range(TILE // CHUNK):
        row = []
        for g in range(SGU_GROUPS):
            row.append(_dot(ws_ref[g], vn[c * CHUNK:(c + 1) * CHUNK, g * CHUNK:(g + 1) * CHUNK]))
        mixed.append(jnp.concatenate(row, axis=1) + bs_ref[...])
    a_br = u * jnp.concatenate(mixed, axis=0)

    r_br = (hf_ref[...] + hb_ref[...]) * _gelu(proj(Z0, LRU_W))

    y = _sigmoid(proj(G0, D_MODEL)) * _dot(a_br.astype(BF16), wa_ref[...])
    y = y + _sigmoid(proj(G0 + D_MODEL, D_MODEL)) * _dot(b_br.astype(BF16), wb_ref[...])
    y = y + _sigmoid(proj(G0 + 2 * D_MODEL, D_MODEL)) * _dot(r_br.astype(BF16), wc_ref[...])
    out = _dot(y.astype(BF16), wo_ref[...]) + bo_ref[...]
    x1 = _layer_norm(ALPHA * x + m[2:3, :] * out, l1g_ref[...], l1b_ref[...])
    x1_ref[...] = x1
    h2 = x1 * (1.0 + m[4:5, :]) + m[3:4, :]
    h2_ref[...] = _pack_rows(h2)

    logits = _dot3(h2, wrt_ref[...]) + brt_ref[...]
    lane = lax.broadcasted_iota(jnp.int32, (TILE, LANES), 1)
    lane_f = lane.astype(F32)
    big = float(LANES)
    gl = jnp.where(lane < N_GROUPS, logits, NEG_INF)
    gmax = jnp.max(gl, axis=-1, keepdims=True)
    g_idx = jnp.min(jnp.where(gl == gmax, lane_f, big), axis=-1, keepdims=True)
    g_w = 1.0 / jnp.sum(jnp.exp(gl - gmax), axis=-1, keepdims=True)
    lo = N_GROUPS + EXPERTS_PER_GROUP * g_idx
    el = jnp.where((lane_f >= lo) & (lane_f < lo + EXPERTS_PER_GROUP), logits, NEG_INF)
    v1 = jnp.max(el, axis=-1, keepdims=True)
    l1 = jnp.min(jnp.where(el == v1, lane_f, big), axis=-1, keepdims=True)
    el2 = jnp.where(lane_f == l1, NEG_INF, el)
    v2 = jnp.max(el2, axis=-1, keepdims=True)
    l2 = jnp.min(jnp.where(el2 == v2, lane_f, big), axis=-1, keepdims=True)
    e21 = jnp.exp(v2 - v1)
    gate1 = g_w / (1.0 + e21)
    gate2 = g_w * e21 / (1.0 + e21)
    e1 = l1 - N_GROUPS
    e2 = l2 - N_GROUPS

    oh1 =jnp.where(lane_f == e1, 1.0, 0.0)
    oh2 = jnp.where(lane_f == e2, 1.0, 0.0)
    both = oh1 + oh2
    tr = lax.broadcasted_iota(jnp.int32, (TILE, TILE), 0)
    tc = lax.broadcasted_iota(jnp.int32, (TILE, TILE), 1)
    earlier = jnp.where(tc < tr, 1.0, 0.0).astype(BF16)
    before = _dot(earlier, both.astype(BF16)) + run_s[...]
    rank1 = jnp.sum(before * oh1, axis=-1, keepdims=True)
    rank2 = jnp.sum(before * oh2, axis=-1, keepdims=True)
    run_s[...] = run_s[...] + jnp.sum(both, axis=0, keepdims=True)
    cnt_ref[...] = jnp.broadcast_to(run_s[...], cnt_ref.shape)

    ri = jnp.where(lane == 0, e1, jnp.where(lane == 1, e2, jnp.where(lane == 2, rank1,
                                                                      jnp.where(lane == 3, rank2, 0.0))))
    ri_ref[...] = ri.astype(jnp.int32)
    rf_ref[...] = jnp.where(lane == 0, gate1, jnp.where(lane == 1, gate2, 0.0))


def _window_bias():
    r = jnp.arange(ATT_HEADS * WINDOW)[:, None] % WINDOW
    c = jnp.arange(CTX_LEN + 3 * WINDOW)[None, :]
    j_prev = c - CTX_LEN
    j_next = c - (CTX_LEN + 2 * WINDOW)
    in_prev = (j_prev >= 0) & (j_prev < WINDOW)
    in_next = j_next >= 0
    full = ~(in_prev & (j_prev < r)) & ~(in_next & (j_next > r))
    ctx_only = jnp.broadcast_to(c < CTX_LEN, full.shape)
    masks = jnp.stack([jnp.stack([full, full & ~in_prev, ctx_only]),
                       jnp.stack([full, full & ~in_next, ctx_only])])
    return jnp.where(masks, 0.0, NEG_INF).astype(F32)


def _mixer(lyr, xc, mods, w_main, b_main, cos_t, sin_t, sink, k, v, hf, hb,
           sg, sb, w_sp, b_sp, wa, wb, wc, wo, bo, l1g, l1b, w_rt, b_rt):
    bsz, n_tok, _ = xc.shape
    nt = n_tok // TILE
    n_half = n_tok // WINDOW
    once = pl.Buffered(1)
    bias = _window_bias()
    n_q, n_keys = bias.shape[2:]
    variant = lambda i, edge: jnp.where(i == 0, 2, jnp.where(i == edge, 1, 0))
    bias0 = pl.BlockSpec((None, None, n_q, n_keys), lambda b, i, ly: (0, variant(i, 1), 0, 0))
    bias1 = pl.BlockSpec((None, None, n_q, n_keys), lambda b, i, ly: (1, variant(i, nt - 1), 0, 0))
    tile = lambda b, i, ly: (b, i, 0)
    lay3 = lambda b, i, ly: (ly[0], 0, 0)
    flat = lambda b, i, ly: (b * nt + i, 0)
    kv_ctx = pl.BlockSpec((None, CTX_LEN, KV_W), lambda b, i, ly: (b, 0, 0))
    kv_prev = pl.BlockSpec((None, WINDOW, KV_W), lambda b, i, ly: (b, jnp.maximum(2 * i - 1, 0), 0))
    kv_this = pl.BlockSpec((None, TILE, KV_W), tile)
    kv_next = pl.BlockSpec((None, WINDOW, KV_W), lambda b, i, ly: (b, jnp.minimum(2 * i + 2, n_half - 1), 0))
    row_vec = lambda width: pl.BlockSpec((None, 1, width), lay3)
    return pl.pallas_call(
        _mixer_kernel,
        out_shape=(jax.ShapeDtypeStruct((bsz, n_tok, D_MODEL), F32),
                   jax.ShapeDtypeStruct((bsz * n_tok, D_MODEL // 2), jnp.uint32),
                   jax.ShapeDtypeStruct((bsz * n_tok, LANES), jnp.int32),
                   jax.ShapeDtypeStruct((bsz * n_tok, LANES), F32),
                   jax.ShapeDtypeStruct((SUBLANES, LANES), F32)),
        grid_spec=pltpu.PrefetchScalarGridSpec(
            num_scalar_prefetch=1,
            grid=(bsz, nt),
            in_specs=[
                pl.BlockSpec((None, TILE, D_MODEL), tile),
                pl.BlockSpec((None, None, N_MOD, D_MODEL),
                             lambda b, i, ly: (ly[0], jnp.where(i == 0, bsz, b), 0, 0)),
                pl.BlockSpec((None, D_MODEL, MAIN_W), lay3, pipeline_mode=once),
                row_vec(MAIN_W),
                pl.BlockSpec((TILE, KV_W), lambda b, i, ly: (i, 0)),
                pl.BlockSpec((TILE, KV_W), lambda b, i, ly: (i, 0)),
                pl.BlockSpec((None, 1, ATT_HEADS), lay3),
                bias0, bias1,
                kv_ctx, kv_ctx, kv_prev, kv_prev, kv_this, kv_this, kv_next, kv_next,
                pl.BlockSpec((None, TILE, LRU_W), tile),
                pl.BlockSpec((None, TILE, LRU_W), tile),
                row_vec(SGU_W), row_vec(SGU_W),
                pl.BlockSpec((None, SGU_GROUPS, CHUNK, CHUNK), lambda b, i, ly: (ly[0], 0, 0, 0)),
                pl.BlockSpec((None, CHUNK, SGU_W), lay3),
                pl.BlockSpec((None, SGU_W, D_MODEL), lay3, pipeline_mode=once),
                pl.BlockSpec((None, ATT_W, D_MODEL), lay3, pipeline_mode=once),
                pl.BlockSpec((None, LRU_W, D_MODEL), lay3, pipeline_mode=once),
                pl.BlockSpec((None, D_MODEL, D_MODEL), lay3, pipeline_mode=once),
                row_vec(D_MODEL), row_vec(D_MODEL), row_vec(D_MODEL),
                pl.BlockSpec((None, D_MODEL, LANES), lay3),
                row_vec(LANES),
            ],
            out_specs=(pl.BlockSpec((None, TILE, D_MODEL), tile),
                       pl.BlockSpec((TILE, D_MODEL // 2), flat),
                       pl.BlockSpec((TILE, LANES), flat),
                       pl.BlockSpec((TILE, LANES), flat),
                       pl.BlockSpec((SUBLANES, LANES), lambda b, i, ly: (0, 0))),
            scratch_shapes=[pltpu.VMEM((1, LANES), F32)],
        ),
        compiler_params=pltpu.CompilerParams(dimension_semantics=("arbitrary", "arbitrary"),
                                             vmem_limit_bytes=VMEM_LIMIT),
        name="mixer",
    )(lyr, xc, mods, w_main, b_main, cos_t, sin_t, sink, bias, bias, k, v, k, v, k, v, k, v, hf, hb,
      sg, sb, w_sp, b_sp, wa, wb, wc, wo, bo, l1g, l1b, w_rt, b_rt)


def _dispatch_kernel(slot_ref, h2_ref, buf_in_ref, buf_ref, tile_s, in_sems, out_sems, *, n_steps):
    del buf_in_ref
    s = pl.program_id(0)

    def load(step, p):
        return pltpu.make_async_copy(h2_ref.at[pl.ds(step * TILE, TILE)], tile_s.at[p], in_sems.at[p])

    def scatters(step, p):
        return [(k, pltpu.make_async_copy(tile_s.at[p, pl.ds(t, 1)],
                                          buf_ref.at[pl.ds(slot_ref[TOP_K * (step * TILE + t) + k], 1)],
                                          out_sems.at[p]))
                for t in range(TILE) for k in range(TOP_K)]

    @pl.when(s == 0)
    def _():
        load(s, 0).start()

    def step_body(p):
        @pl.when(s >= 1)
        def _():
            for _, cp in scatters(s - 1, 1 - p):
                cp.wait()

        @pl.when(s + 1 < n_steps)
        def _():
            load(s + 1, 1 - p).start()

        load(s, p).wait()
        for k, cp in scatters(s, p):
            cp.start(priority=k)

        @pl.when(s == n_steps - 1)
        def _():
            for _, cp in scatters(s, p):
                cp.wait()

    for p in range(2):
        pl.when(s % 2 == p)(functools.partial(step_body, p))


def _dispatch(slot, h2, n_slots):
    n_tok, width = h2.shape
    n_steps = n_tok // TILE
    return pl.pallas_call(
        functools.partial(_dispatch_kernel, n_steps=n_steps),
        out_shape=jax.ShapeDtypeStruct((n_slots, width), h2.dtype),
        grid_spec=pltpu.PrefetchScalarGridSpec(
            num_scalar_prefetch=1,
            grid=(n_steps,),
            in_specs=[pl.BlockSpec(memory_space=pl.ANY), pl.BlockSpec(memory_space=pl.ANY)],
            out_specs=pl.BlockSpec(memory_space=pl.ANY),
            scratch_shapes=[pltpu.VMEM((2, TILE, width), h2.dtype),
                            pltpu.SemaphoreType.DMA((2,)), pltpu.SemaphoreType.DMA((2,))],
        ),
        input_output_aliases={2: 0},
        compiler_params=pltpu.CompilerParams(dimension_semantics=("arbitrary",)),
        name="dispatch",
    )(slot, h2, jnp.zeros((n_slots, width), h2.dtype))


def _experts_kernel(ly_ref, be_ref, nu_ref, x_ref, w1_ref, w3_ref, w2_ref, o_ref, w1s, w3s, w2s):
    del ly_ref
    i = pl.program_id(0)
    changed = (i == 0) | (be_ref[i] != be_ref[jnp.maximum(i - 1, 0)])

    @pl.when((i < nu_ref[0]) & changed)
    def _():
        w1s[...] = w1_ref[...].astype(BF16)
        w3s[...] = w3_ref[...].astype(BF16)
        w2s[...] = w2_ref[...].astype(BF16)

    @pl.when(i < nu_ref[0])
    def _():
        x = _unpack_rows(x_ref[...]).astype(BF16)
        h1 = _dot(x, w1s[...])
        h3 = _dot(x, w3s[...])
        act = (h1 * _sigmoid(h1) * h3).astype(BF16)
        o_ref[...] = _pack_rows(_dot(act, w2s[...]))

    @pl.when(i >= nu_ref[0])
    def _():
        o_ref[...] = jnp.zeros_like(o_ref)


def _experts(lyr, block_expert, n_used, buf, w1, w3, w2):
    n_blocks = buf.shape[0] // EXPERT_BLOCK

    def blk(i, ly, be, nu):
        return (jnp.minimum(i, nu[0] - 1), 0)

    def wsel(i, ly, be, nu):
        return (ly[0], be[jnp.minimum(i, nu[0] - 1)], 0, 0)

    return pl.pallas_call(
        _experts_kernel,
        out_shape=jax.ShapeDtypeStruct(buf.shape, buf.dtype),
        grid_spec=pltpu.PrefetchScalarGridSpec(
            num_scalar_prefetch=3,
            grid=(n_blocks,),
            in_specs=[pl.BlockSpec((EXPERT_BLOCK, D_MODEL // 2), blk),
                      pl.BlockSpec((None, None, D_MODEL, D_EXPERT), wsel),
                      pl.BlockSpec((None, None, D_MODEL, D_EXPERT), wsel),
                      pl.BlockSpec((None, None, D_EXPERT, D_MODEL), wsel)],
            out_specs=pl.BlockSpec((EXPERT_BLOCK, D_MODEL // 2), lambda i, ly, be, nu: (i, 0)),
            scratch_shapes=[pltpu.VMEM((D_MODEL, D_EXPERT), BF16),
                            pltpu.VMEM((D_MODEL, D_EXPERT), BF16),
                            pltpu.VMEM((D_EXPERT, D_MODEL), BF16)],
        ),
        compiler_params=pltpu.CompilerParams(dimension_semantics=("arbitrary",),
                                             vmem_limit_bytes=40 * MIB),
        name="experts",
    )(lyr, block_expert, n_used, buf, w1, w3, w2)


def _combine_kernel(ly_ref, slot_ref, x1_ref, mod_ref, rf_ref, g_ref, b_ref, eo_ref, o_ref, gbuf, sems,
                    *, n_steps):
    del ly_ref
    s = pl.program_id(0)
    n_groups = TILE // GATHER_ROWS
    gate2 = mod_ref[5:6, :]
    ln_g, ln_b = g_ref[...], b_ref[...]

    def group_copies(tile, buf, j):
        copies = []
        for r in range(GATHER_ROWS):
            t = j * GATHER_ROWS + r
            for k in range(TOP_K):
                src = eo_ref.at[pl.ds(slot_ref[TOP_K * (tile * TILE + t) + k], 1)]
                copies.append((k, pltpu.make_async_copy(src, gbuf.at[buf, k, pl.ds(t, 1)], sems.at[buf, j])))
        return copies

    def start_group(tile, buf, j):
        for k, cp in group_copies(tile, buf, j):
            cp.start(priority=k)

    def wait_group(tile, buf, j):
        for _, cp in group_copies(tile, buf, j):
            cp.wait()

    @pl.when(s == 0)
    def _():
        for j in range(n_groups):
            start_group(s, 0, j)

    nxt = jnp.minimum(s + 1, n_steps - 1)

    def step_body(p):
        for j in range(n_groups):
            start_group(nxt, 1 - p, j)
            wait_group(s, p, j)
            rows = slice(j * GATHER_ROWS, (j + 1) * GATHER_ROWS)
            rf = rf_ref[rows, :]
            y = (rf[:, 0:1] * _unpack_rows(gbuf[p, 0, rows, :])
                 + rf[:, 1:2] * _unpack_rows(gbuf[p, 1, rows, :]))
            o_ref[rows, :] = _layer_norm(ALPHA * x1_ref[rows, :] + gate2 * y, ln_g, ln_b)

        @pl.when(s == n_steps - 1)
        def _():
            for j in range(n_groups):
                wait_group(nxt, 1 - p, j)

    for p in range(2):
        pl.when(s % 2 == p)(functools.partial(step_body, p))


def _combine(lyr, slot, x1, mods, rf, l2g, l2b, eo, latent_only):
    bsz, n_tok, _ = x1.shape
    nt = n_tok // TILE
    n_steps = bsz * nt
    tile = lambda s, ly, sl: (s // nt, s % nt, 0)
    lay3 = lambda s, ly, sl: (ly[0], 0, 0)
    if latent_only:
        out_rows = n_tok - CTX_LEN
        out_tile = lambda s, ly, sl: (s // nt, jnp.maximum(s % nt - 1, 0), 0)
    else:
        out_rows, out_tile = n_tok, tile
    return pl.pallas_call(
        functools.partial(_combine_kernel, n_steps=n_steps),
        out_shape=jax.ShapeDtypeStruct((bsz, out_rows, D_MODEL), F32),
        grid_spec=pltpu.PrefetchScalarGridSpec(
            num_scalar_prefetch=2,
            grid=(n_steps,),
            in_specs=[pl.BlockSpec((None, TILE, D_MODEL), tile),
                      pl.BlockSpec((None, None, N_MOD, D_MODEL),
                                   lambda s, ly, sl: (ly[0], jnp.where(s % nt == 0, bsz, s // nt), 0, 0)),
                      pl.BlockSpec((TILE, LANES), lambda s, ly, sl: (s, 0)),
                      pl.BlockSpec((None, 1, D_MODEL), lay3),
                      pl.BlockSpec((None, 1, D_MODEL), lay3),
                      pl.BlockSpec(memory_space=pl.ANY)],
            out_specs=pl.BlockSpec((None, TILE, D_MODEL), out_tile),
            scratch_shapes=[pltpu.VMEM((2, TOP_K, TILE, D_MODEL // 2), jnp.uint32),
                            pltpu.SemaphoreType.DMA((2, TILE // GATHER_ROWS))],
        ),
        compiler_params=pltpu.CompilerParams(dimension_semantics=("arbitrary",),
                                             vmem_limit_bytes=40 * MIB),
        name="combine",
    )(lyr, slot, x1, mods, rf, l2g, l2b, eo)


def _slot_tables(route_i, counts, n_blocks):
    counts = counts.astype(jnp.int32)
    padded = (counts + EXPERT_BLOCK - 1) // EXPERT_BLOCK * EXPERT_BLOCK
    pad_ends = jnp.cumsum(padded)
    pad_starts = pad_ends - padded
    expert = route_i[:, :TOP_K]
    rank = route_i[:, TOP_K:2 * TOP_K]
    is_e = expert[..., None] == jnp.arange(N_EXPERTS, dtype=jnp.int32)
    slot = (jnp.sum(jnp.where(is_e, pad_starts, 0), axis=-1) + rank).reshape(-1)
    block_row = jnp.arange(n_blocks, dtype=jnp.int32) * EXPERT_BLOCK
    block_expert = jnp.minimum(jnp.sum((pad_ends[None, :] <= block_row[:, None]).astype(jnp.int32), axis=1),
                               N_EXPERTS - 1)
    n_used = (pad_ends[-1:] // EXPERT_BLOCK).astype(jnp.int32)
    return slot, block_expert, n_used


def _rope_tables(n_lat):
    rows = n_lat // GRID_W
    row = jnp.repeat(jnp.arange(rows, dtype=F32), GRID_W)
    col = jnp.tile(jnp.arange(GRID_W, dtype=F32), rows)
    axis_dim = HEAD_DIM // 2
    inv_freq = ROPE_THETA ** (-jnp.arange(0, axis_dim, 2, dtype=F32) / axis_dim)
    ang_r = row[:, None] * inv_freq
    ang_c = col[:, None] * inv_freq
    cos = jnp.concatenate([jnp.cos(ang_r)] * 2 + [jnp.cos(ang_c)] * 2, axis=1)
    sin = jnp.concatenate([-jnp.sin(ang_r), jnp.sin(ang_r), -jnp.sin(ang_c), jnp.sin(ang_c)], axis=1)
    cos = jnp.concatenate([jnp.ones((CTX_LEN, HEAD_DIM), F32), cos], axis=0)
    sin = jnp.concatenate([jnp.zeros((CTX_LEN, HEAD_DIM), F32), sin], axis=0)
    return jnp.tile(cos, (1, KV_HEADS)), jnp.tile(sin, (1, KV_HEADS))


def kernel(x, c, ctx, c_ctx, w_mod, b_mod, w_in, b_in, sgu_ln_g, sgu_ln_b, w_spatial, b_spatial, attn_sink, conv_w, conv_b, w_rgate, b_rgate, w_igate, b_igate, lru_lambda, w_proj_a, w_proj_b, w_proj_c, w_out, b_out, ln1_g, ln1_b, ln2_g, ln2_b, w_group, b_group, w_router, b_router, w1, w3, w2):
    bsz, n_lat, _ = x.shape
    n_layers = w_mod.shape[0]
    assert ctx.shape[1] == CTX_LEN == TILE and n_lat % TILE == 0 and bsz < SUBLANES
    n_tok = CTX_LEN + n_lat
    n_blocks = -(-bsz * n_tok * TOP_K // EXPERT_BLOCK) + N_EXPERTS
    n_slots = n_blocks * EXPERT_BLOCK

    cond = jnp.zeros((SUBLANES, D_MODEL), F32).at[:bsz].set(c).at[bsz].set(c_ctx)
    mods = _modulation(cond, w_mod, b_mod).reshape(n_layers, SUBLANES, N_MOD, D_MODEL)
    w_main = w_in[:, :, :MAIN_W].astype(BF16)
    w_ctx = w_in[:, :, MAIN_W:].astype(BF16)
    b_main = b_in[:, None, :MAIN_W]
    b_ctx = b_in[:, None, MAIN_W:]
    w_ri = jnp.concatenate([w_rgate, w_igate], axis=-1).astype(BF16)
    w_sp = w_spatial.astype(BF16)
    b_sp = jnp.repeat(jnp.swapaxes(b_spatial, 1, 2), CHUNK, axis=2)
    wa, wb, wc, wo = (w.astype(BF16) for w in (w_proj_a, w_proj_b, w_proj_c, w_out))
    w_rt = jnp.zeros((n_layers, D_MODEL, LANES), F32)
    w_rt = w_rt.at[:, :, :N_GROUPS].set(w_group).at[:, :, N_GROUPS:N_GROUPS + N_EXPERTS].set(w_router)
    b_rt = jnp.zeros((n_layers, 1, LANES), F32)
    b_rt = b_rt.at[:, 0, :N_GROUPS].set(b_group).at[:, 0, N_GROUPS:N_GROUPS + N_EXPERTS].set(b_router)
    cos_t, sin_t = _rope_tables(n_lat)
    row3 = lambda a: a[:, None, :]
    conv_b4, b_r4, b_i4 = (a[:, :, None, :] for a in (conv_b, b_rgate, b_igate))
    sink, sg, sb, bo = row3(attn_sink), row3(sgu_ln_g), row3(sgu_ln_b), row3(b_out)
    l1g, l1b, l2g, l2b = row3(ln1_g), row3(ln1_b), row3(ln2_g), row3(ln2_b)

    def layer(l, xc):
        lyr = jnp.full((1,), l, jnp.int32)
        k, v, hf, hb = _kv_lru(lyr, xc, mods, w_ctx, b_ctx, cos_t, sin_t, conv_w, conv_b4,
                               w_ri, b_r4, b_i4, lru_lambda)
        x1, h2, route_i, route_f, counts = _mixer(
            lyr, xc, mods, w_main, b_main, cos_t, sin_t, sink, k, v, hf, hb,
            sg, sb, w_sp, b_sp, wa, wb, wc, wo, bo, l1g, l1b, w_rt, b_rt)
        slot, block_expert, n_used = _slot_tables(route_i, counts[0, :N_EXPERTS], n_blocks)
        buf = _dispatch(slot, h2, n_slots)
        eo = _experts(lyr, block_expert, n_used, buf, w1, w3, w2)
        return _combine(lyr, slot, x1, mods, route_f, l2g, l2b, eo, latent_only=l == n_layers - 1)

    xc = jnp.concatenate([ctx, x], axis=1)
    for l in range(n_layers):
        xc = layer(l, xc)
    return xc
```

```python
import functools

import jax
import jax.numpy as jnp
from jax import lax
from jax.experimental import pallas as pl
from jax.experimental.pallas import tpu as pltpu

F32 = jnp.float32
BF16 = jnp.bfloat16

D_MODEL = 1024
DEPTH = 4
GRID_W = 64
CTX_LEN = 256
HEAD_DIM = 64
ATT_HEADS = 8
KV_HEADS = 2
Q_PER_KV = ATT_HEADS // KV_HEADS
ATT_W = ATT_HEADS * HEAD_DIM
KV_W = KV_HEADS * HEAD_DIM
WINDOW = 128
ROPE_THETA = 10000.0
CHUNK = 128
SGU_GROUPS = 4
SGU_W = SGU_GROUPS * CHUNK
LRU_W = D_MODEL
LRU_BLOCKS = 8
LRU_BW = LRU_W // LRU_BLOCKS
CONV_W = 4
LRU_C = 8.0
N_GROUPS = 4
EXPERTS_PER_GROUP = 8
N_EXPERTS = N_GROUPS * EXPERTS_PER_GROUP
TOP_K = 2
D_EXPERT = 512
N_MOD = 6
ALPHA = (2.0 * DEPTH) ** 0.25
LN_EPS = 1e-6
NEG_INF = -1e30

Q0, U0, V0, G0, Z0 = 0, ATT_W, ATT_W + SGU_W, ATT_W + 2 * SGU_W, ATT_W + 2 * SGU_W + 3 * D_MODEL
MAIN_W = Z0 + LRU_W
CTX_W = 2 * KV_W + LRU_W

TILE = 256
SUBLANES = 8
LANES = 128
EXPERT_BLOCK = 512
GATHER_ROWS = 128
MIB = 1024 * 1024
VMEM_LIMIT = 56 * MIB


def _sigmoid(x):
    return 0.5 * (1.0 + jnp.tanh(0.5 * x))


def _gelu(x):
    return 0.5 * x * (1.0 + jnp.tanh(0.7978845608028654 * (x + 0.044715 * (x * x * x))))


def _layer_norm(x, g, b):
    mu = jnp.mean(x, axis=-1, keepdims=True)
    xc = x - mu
    var = jnp.mean(xc * xc, axis=-1, keepdims=True)
    return xc * lax.rsqrt(var + LN_EPS) * g + b


def _dot(a, b):
    return jnp.dot(a, b, preferred_element_type=F32)


def _dot3(a, b):
    a_hi = a.astype(BF16)
    a_lo = (a - a_hi.astype(F32)).astype(BF16)
    b_hi = b.astype(BF16)
    b_lo = (b - b_hi.astype(F32)).astype(BF16)
    return _dot(a_hi, b_hi) + (_dot(a_hi, b_lo) + _dot(a_lo, b_hi))


def _pack_rows(v):
    n = v.shape[1] // 2
    bits = pltpu.bitcast(v.astype(BF16).astype(F32), jnp.uint32)
    return (bits[:, :n] >> 16) | bits[:, n:]


def _unpack_rows(p):
    lo = pltpu.bitcast(p << 16, F32)
    hi = pltpu.bitcast(p & jnp.uint32(0xFFFF0000), F32)
    return jnp.concatenate([lo, hi], axis=1)


def _rope(x, cos, sin_signed):
    n = x.shape[1] // LANES
    if n > 1:
        cos = jnp.concatenate([cos] * n, axis=1)
        sin_signed = jnp.concatenate([sin_signed] * n, axis=1)
    lane = lax.broadcasted_iota(jnp.int32, x.shape, 1)
    first = (lane & 31) < 16
    w = x.shape[1]
    partner = jnp.where(first, pltpu.roll(x, w - 16, 1), pltpu.roll(x, 16, 1))
    return x * cos + partner * sin_signed


def _mods_kernel(c_ref, w_ref, b_ref, o_ref):
    c = c_ref[...]
    o_ref[...] = _dot3(c * _sigmoid(c), w_ref[...]) + b_ref[...]


def _modulation(cond, w_mod, b_mod):
    n_layers = w_mod.shape[0]
    tn = 1536
    return pl.pallas_call(
        _mods_kernel,
        out_shape=jax.ShapeDtypeStruct((n_layers, SUBLANES, N_MOD * D_MODEL), F32),
        grid=(n_layers, N_MOD * D_MODEL // tn),
        in_specs=[
            pl.BlockSpec((SUBLANES, D_MODEL), lambda l, j: (0, 0)),
            pl.BlockSpec((None, D_MODEL, tn), lambda l, j: (l, 0, j)),
            pl.BlockSpec((None, 1, tn), lambda l, j: (l, 0, j)),
        ],
        out_specs=pl.BlockSpec((None, SUBLANES, tn), lambda l, j: (l, 0, j)),
        compiler_params=pltpu.CompilerParams(dimension_semantics=("arbitrary", "arbitrary"),
                                             vmem_limit_bytes=40 * MIB),
        name="modulation",
    )(cond, w_mod, b_mod.reshape(n_layers, 1, N_MOD * D_MODEL))


def _scan_block(a_ref, u_ref, out_ref, blk, carry, keeps, reverse):
    r0 = pl.multiple_of(blk * SUBLANES, SUBLANES)
    new_carry = []
    for c in range(a_ref.shape[1] // LANES):
        cols = slice(c * LANES, (c + 1) * LANES)
        a = a_ref[pl.ds(r0, SUBLANES), cols]
        u = u_ref[pl.ds(r0, SUBLANES), cols]
        for s, keep in zip((1, 2, 4), keeps):
            shift = SUBLANES - s if reverse else s
            a_sh = jnp.where(keep, pltpu.roll(a, shift, 0), 1.0)
            u_sh = jnp.where(keep, pltpu.roll(u, shift, 0), 0.0)
            u = a * u_sh + u
            a = a * a_sh
        h = u + a * carry[:, cols]
        out_ref[pl.ds(r0, SUBLANES), cols] = h
        new_carry.append(h[0:1, :] if reverse else h[SUBLANES - 1:SUBLANES, :])
    return jnp.concatenate(new_carry, axis=1)


def _scan_tiles(af, uf, cf, hf_ref, ab, ub, cb, hb_ref):
    n_blk = af.shape[0] // SUBLANES
    row = lax.broadcasted_iota(jnp.int32, (SUBLANES, LANES), 0)
    keep_f = [row >= s for s in (1, 2, 4)]
    keep_b = [row < SUBLANES - s for s in (1, 2, 4)]

    def body(j, carry):
        return (_scan_block(af, uf, hf_ref, j, carry[0], keep_f, reverse=False),
                _scan_block(ab, ub, hb_ref, n_blk - 1 - j, carry[1], keep_b, reverse=True))

    cf[...], cb[...] = lax.fori_loop(0, n_blk, body, (cf[...], cb[...]))


def _lru_inputs(t, wri_ref, br_ref, bi_ref, hl, a_ref, u_ref):
    for n in range(LRU_BLOCKS):
        cols = slice(n * LRU_BW, (n + 1) * LRU_BW)
        tn = t[:, cols]
        g = _dot(tn.astype(BF16), wri_ref[n])
        tr = jnp.tanh(0.5 * (g[:, :LRU_BW] + br_ref[:, cols]))
        ti = jnp.tanh(0.5 * (g[:, LRU_BW:] + bi_ref[:, cols]))
        half_log = hl[:, cols]
        a = jnp.exp(half_log + half_log * tr)
        a_ref[:, cols] = a
        u_ref[:, cols] = jnp.sqrt(1.0 - a * a) * ((0.5 * tn) * (1.0 + ti))


def _kv_lru_kernel(ly_ref, xf_ref, xb_ref, mod_ref, wc_ref, bc_ref, cos_ref, sin_ref,
                   cw_ref, cb_ref, wri_ref, br_ref, bi_ref, lam_ref,
                   k_ref, v_ref, hf_ref, hb_ref,
                   xsf, xsb, af, uf, ab, ub, cf, cbk):
    del ly_ref
    i = pl.program_id(1)
    m = mod_ref[...]
    shift, scale = m[0:1, :], 1.0 + m[1:2, :]

    @pl.when(i == 0)
    def _():
        cf[...] = jnp.zeros_like(cf)
        cbk[...] = jnp.zeros_like(cbk)
        xsf[...] = jnp.zeros_like(xsf)
        xsb[...] = jnp.zeros_like(xsb)

    h = (xf_ref[...] * scale + shift).astype(BF16)
    p = _dot(h, wc_ref[...]) + bc_ref[...]
    k_ref[...] = _rope(p[:, :KV_W], cos_ref[...], sin_ref[...]).astype(BF16)
    v_ref[...] = p[:, KV_W:2 * KV_W].astype(BF16)
    xr_f = p[:, 2 * KV_W:]
    h = (xb_ref[...] * scale + shift).astype(BF16)
    xr_b = _dot(h, wc_ref[:, 2 * KV_W:]) + bc_ref[:, 2 * KV_W:]

    inside = jnp.broadcast_to(i > 1, (SUBLANES, LRU_W))
    xsf[0:SUBLANES, :] = jnp.where(inside, xsf[TILE:TILE + SUBLANES, :], 0.0)
    xsb[TILE:TILE + SUBLANES, :] = jnp.where(inside, xsb[0:SUBLANES, :], 0.0)

    xsf[SUBLANES:SUBLANES + TILE, :] = xr_f
    xsb[0:TILE, :] = xr_b

    lam = lam_ref[...]
    sp = (-0.5 * LRU_C) * (jnp.maximum(-lam, 0.0) + jnp.log1p(jnp.exp(-jnp.abs(lam))))

    cw = cw_ref[0]
    t = (cb_ref[0] + cw[0:1, :] * xsf[SUBLANES - 3:SUBLANES - 3 + TILE, :]
         + cw[1:2, :] * xsf[SUBLANES - 2:SUBLANES - 2 + TILE, :]
         + cw[2:3, :] * xsf[SUBLANES - 1:SUBLANES - 1 + TILE, :]
         + cw[3:4, :] * xr_f)
    _lru_inputs(t, wri_ref.at[0], br_ref.at[0], bi_ref.at[0], sp[0:1, :], af, uf)
    cw = cw_ref[1]
    t = (cb_ref[1] + cw[0:1, :] * xr_b
         + cw[1:2, :] * xsb[1:1 + TILE, :]
         + cw[2:3, :] * xsb[2:2 + TILE, :]
         + cw[3:4, :] * xsb[3:3 + TILE, :])
    _lru_inputs(t, wri_ref.at[1], br_ref.at[1], bi_ref.at[1], sp[1:2, :], ab, ub)

    _scan_tiles(af, uf, cf, hf_ref, ab, ub, cbk, hb_ref)


def _kv_lru(lyr, xc, mods, w_ctx, b_ctx, cos_t, sin_t, conv_w, conv_b, w_ri, b_r, b_i, lam):
    bsz, n_tok, _ = xc.shape
    nt = n_tok // TILE
    tile_f = lambda b, i, ly: (b, i, 0)
    tile_b = lambda b, i, ly: (b, jnp.where(i == 0, 0, nt - i), 0)
    lay3 = lambda b, i, ly: (ly[0], 0, 0)
    lay4 = lambda b, i, ly: (ly[0], 0, 0, 0)
    return pl.pallas_call(
        _kv_lru_kernel,
        out_shape=(jax.ShapeDtypeStruct((bsz, n_tok, KV_W), BF16),
                   jax.ShapeDtypeStruct((bsz, n_tok, KV_W), BF16),
                   jax.ShapeDtypeStruct((bsz, n_tok, LRU_W), F32),
                   jax.ShapeDtypeStruct((bsz, n_tok, LRU_W), F32)),
        grid_spec=pltpu.PrefetchScalarGridSpec(
            num_scalar_prefetch=1,
            grid=(bsz, nt),
            in_specs=[
                pl.BlockSpec((None, TILE, D_MODEL), tile_f),
                pl.BlockSpec((None, TILE, D_MODEL), tile_b),
                pl.BlockSpec((None, None, N_MOD, D_MODEL),
                             lambda b, i, ly: (ly[0], jnp.where(i == 0, bsz, b), 0, 0)),
                pl.BlockSpec((None, D_MODEL, CTX_W), lay3),
                pl.BlockSpec((None, 1, CTX_W), lay3),
                pl.BlockSpec((TILE, KV_W), lambda b, i, ly: (i, 0)),
                pl.BlockSpec((TILE, KV_W), lambda b, i, ly: (i, 0)),
                pl.BlockSpec((None, 2, CONV_W, LRU_W), lay4),
                pl.BlockSpec((None, 2, 1, LRU_W), lay4),
                pl.BlockSpec((None, 2, LRU_BLOCKS, LRU_BW, 2 * LRU_BW), lambda b, i, ly: (ly[0], 0, 0, 0, 0)),
                pl.BlockSpec((None, 2, 1, LRU_W), lay4),
                pl.BlockSpec((None, 2, 1, LRU_W), lay4),
                pl.BlockSpec((None, 2, LRU_W), lay3),
            ],
            out_specs=(pl.BlockSpec((None, TILE, KV_W), tile_f),
                       pl.BlockSpec((None, TILE, KV_W), tile_f),
                       pl.BlockSpec((None, TILE, LRU_W), tile_f),
                       pl.BlockSpec((None, TILE, LRU_W), tile_b)),
            scratch_shapes=[pltpu.VMEM((TILE + SUBLANES, LRU_W), F32),
                            pltpu.VMEM((TILE + SUBLANES, LRU_W), F32),
                            pltpu.VMEM((TILE, LRU_W), F32), pltpu.VMEM((TILE, LRU_W), F32),
                            pltpu.VMEM((TILE, LRU_W), F32), pltpu.VMEM((TILE, LRU_W), F32),
                            pltpu.VMEM((1, LRU_W), F32), pltpu.VMEM((1, LRU_W), F32)],
        ),
        compiler_params=pltpu.CompilerParams(dimension_semantics=("arbitrary", "arbitrary"),
                                             vmem_limit_bytes=VMEM_LIMIT),
        name="kv_lru",
    )(lyr, xc, xc, mods, w_ctx, b_ctx, cos_t, sin_t, conv_w, conv_b, w_ri, b_r, b_i, lam)


def _attention(q, k_all, v_all, bias, sink):
    rows = q.shape[0]
    lane = lax.broadcasted_iota(jnp.int32, (rows, LANES), 1)
    low = lane < HEAD_DIM
    parts = []
    for hd in range(ATT_HEADS):
        g = hd // Q_PER_KV
        t = q[:, (hd // 2) * LANES:(hd // 2 + 1) * LANES]
        if hd % 2 != g:
            t = pltpu.roll(t, HEAD_DIM, 1)
        parts.append(jnp.where(low if g == 0 else ~low, t, 0.0))
    qs = jnp.concatenate(parts, axis=0).astype(BF16)
    s = lax.dot_general(qs, k_all, (((1,), (1,)), ((), ())), preferred_element_type=F32)
    s = s + bias
    blk = lax.broadcasted_iota(jnp.int32, (ATT_HEADS * rows, 1), 0)
    sink_col = jnp.zeros((ATT_HEADS * rows, 1), F32)
    for hd in range(ATT_HEADS):
        sink_col = jnp.where((blk >= hd * rows) & (blk < (hd + 1) * rows), sink[:, hd:hd + 1], sink_col)
    mx = jnp.maximum(jnp.max(s, axis=-1, keepdims=True), sink_col)
    p = jnp.exp(s - mx)
    den = jnp.sum(p, axis=-1, keepdims=True) + jnp.exp(sink_col - mx)
    o = _dot(p.astype(BF16), v_all) * (1.0 / den)
    tiles = []
    for t in range(ATT_HEADS // 2):
        g = (2 * t) // Q_PER_KV
        even = o[2 * t * rows:(2 * t + 1) * rows, :]
        odd = o[(2 * t + 1) * rows:(2 * t + 2) * rows, :]
        if g == 1:
            even = pltpu.roll(even, HEAD_DIM, 1)
        else:
            odd = pltpu.roll(odd, HEAD_DIM, 1)
        tiles.append(jnp.where(low, even, odd))
    return jnp.concatenate(tiles, axis=1)


def _mixer_kernel(ly_ref, x_ref, mod_ref, w_ref, b_ref, cos_ref, sin_ref, sink_ref, bias0_ref, bias1_ref,
                  kc_ref, vc_ref, kp_ref, vp_ref, kt_ref, vt_ref, kn_ref, vn_ref,
                  hf_ref, hb_ref, sg_ref, sb_ref, ws_ref, bs_ref,
                  wa_ref, wb_ref, wc_ref, wo_ref, bo_ref, l1g_ref, l1b_ref, wrt_ref, brt_ref,
                  x1_ref, h2_ref, ri_ref, rf_ref, cnt_ref,
                  run_s):
    del ly_ref
    @pl.when((pl.program_id(0) == 0) & (pl.program_id(1) == 0))
    def _():
        run_s[...] = jnp.zeros_like(run_s)

    x = x_ref[...]
    m = mod_ref[...]
    h = (x * (1.0 + m[1:2, :]) + m[0:1, :]).astype(BF16)

    def proj(c0, width):
        return _dot(h, w_ref[:, c0:c0 + width]) + b_ref[:, c0:c0 + width]

    q = _rope(proj(Q0, ATT_W), cos_ref[...], sin_ref[...]) * (HEAD_DIM ** -0.5)
    sink = sink_ref[...]

    kt, vt = kt_ref[...], vt_ref[...]
    att = []
    for half in range(2):
        if half == 0:
            k_all = jnp.concatenate([kc_ref[...], kp_ref[...], kt], axis=0)
            v_all = jnp.concatenate([vc_ref[...], vp_ref[...], vt], axis=0)
            bias = bias0_ref[...]
        else:
            k_all = jnp.concatenate([kc_ref[...], kt, kn_ref[...]], axis=0)
            v_all = jnp.concatenate([vc_ref[...], vt, vn_ref[...]], axis=0)
            bias = bias1_ref[...]
        qh = q[half * WINDOW:(half + 1) * WINDOW, :]
        att.append(_attention(qh, k_all, v_all, bias, sink))
    b_br = jnp.concatenate(att, axis=0)

    u = _gelu(proj(U0, SGU_W))
    vn = _layer_norm(_gelu(proj(V0, SGU_W)), sg_ref[...], sb_ref[...]).astype(BF16)
    mixed = []
    for c in range(TILE // CHUNK):
        row = []
        for g in range(SGU_GROUPS):
            row.append(_dot(ws_ref[g], vn[c * CHUNK:(c + 1) * CHUNK, g * CHUNK:(g + 1) * CHUNK]))
        mixed.append(jnp.concatenate(row, axis=1) + bs_ref[...])
    a_br = u * jnp.concatenate(mixed, axis=0)

    r_br = (hf_ref[...] + hb_ref[...]) * _gelu(proj(Z0, LRU_W))

    y = _sigmoid(proj(G0, D_MODEL)) * _dot(a_br.astype(BF16), wa_ref[...])
    y = y + _sigmoid(proj(G0 + D_MODEL, D_MODEL)) * _dot(b_br.astype(BF16), wb_ref[...])
    y = y + _sigmoid(proj(G0 + 2 * D_MODEL, D_MODEL)) * _dot(r_br.astype(BF16), wc_ref[...])
    out = _dot(y.astype(BF16), wo_ref[...]) + bo_ref[...]
    x1 = _layer_norm(ALPHA * x + m[2:3, :] * out, l1g_ref[...], l1b_ref[...])
    x1_ref[...] = x1
    h2 = x1 * (1.0 + m[4:5, :]) + m[3:4, :]
    h2_ref[...] = _pack_rows(h2)

    logits = _dot3(h2, wrt_ref[...]) + brt_ref[...]
    lane = lax.broadcasted_iota(jnp.int32, (TILE, LANES), 1)
    lane_f = lane.astype(F32)
    big = float(LANES)
    gl = jnp.where(lane < N_GROUPS, logits, NEG_INF)
    gmax = jnp.max(gl, axis=-1, keepdims=True)
    g_idx = jnp.min(jnp.where(gl == gmax, lane_f, big), axis=-1, keepdims=True)
    g_w = 1.0 / jnp.sum(jnp.exp(gl - gmax), axis=-1, keepdims=True)
    lo = N_GROUPS + EXPERTS_PER_GROUP * g_idx
    el = jnp.where((lane_f >= lo) & (lane_f < lo + EXPERTS_PER_GROUP), logits, NEG_INF)
    v1 = jnp.max(el, axis=-1, keepdims=True)
    l1 = jnp.min(jnp.where(el == v1, lane_f, big), axis=-1, keepdims=True)
    el2 = jnp.where(lane_f == l1, NEG_INF, el)
    v2 = jnp.max(el2, axis=-1, keepdims=True)
    l2 = jnp.min(jnp.where(el2 == v2, lane_f, big), axis=-1, keepdims=True)
    e21 = jnp.exp(v2 - v1)
    gate1 = g_w / (1.0 + e21)
    gate2 = g_w * e21 / (1.0 + e21)
    e1 = l1 - N_GROUPS
    e2 = l2 - N_GROUPS

    oh1 =jnp.where(lane_f == e1, 1.0, 0.0)
    oh2 = jnp.where(lane_f == e2, 1.0, 0.0)
    both = oh1 + oh2
    tr = lax.broadcasted_iota(jnp.int32, (TILE, TILE), 0)
    tc = lax.broadcasted_iota(jnp.int32, (TILE, TILE), 1)
    earlier = jnp.where(tc < tr, 1.0, 0.0).astype(BF16)
    before = _dot(earlier, both.astype(BF16)) + run_s[...]
    rank1 = jnp.sum(before * oh1, axis=-1, keepdims=True)
    rank2 = jnp.sum(before * oh2, axis=-1, keepdims=True)
    run_s[...] = run_s[...] + jnp.sum(both, axis=0, keepdims=True)
    cnt_ref[...] = jnp.broadcast_to(run_s[...], cnt_ref.shape)

    ri = jnp.where(lane == 0, e1, jnp.where(lane == 1, e2, jnp.where(lane == 2, rank1,
                                                                      jnp.where(lane == 3, rank2, 0.0))))
    ri_ref[...] = ri.astype(jnp.int32)
    rf_ref[...] = jnp.where(lane == 0, gate1, jnp.where(lane == 1, gate2, 0.0))


def _window_bias():
    r = jnp.arange(ATT_HEADS * WINDOW)[:, None] % WINDOW
    c = jnp.arange(CTX_LEN + 3 * WINDOW)[None, :]
    j_prev = c - CTX_LEN
    j_next = c - (CTX_LEN + 2 * WINDOW)
    in_prev = (j_prev >= 0) & (j_prev < WINDOW)
    in_next = j_next >= 0
    full = ~(in_prev & (j_prev < r)) & ~(in_next & (j_next > r))
    ctx_only = jnp.broadcast_to(c < CTX_LEN, full.shape)
    masks = jnp.stack([jnp.stack([full, full & ~in_prev, ctx_only]),
                       jnp.stack([full, full & ~in_next, ctx_only])])
    return jnp.where(masks, 0.0, NEG_INF).astype(F32)


def _mixer(lyr, xc, mods, w_main, b_main, cos_t, sin_t, sink, k, v, hf, hb,
           sg, sb, w_sp, b_sp, wa, wb, wc, wo, bo, l1g, l1b, w_rt, b_rt):
    bsz, n_tok, _ = xc.shape
    nt = n_tok // TILE
    n_half = n_tok // WINDOW
    once = pl.Buffered(1)
    bias = _window_bias()
    n_q, n_keys = bias.shape[2:]
    variant = lambda i, edge: jnp.where(i == 0, 2, jnp.where(i == edge, 1, 0))
    bias0 = pl.BlockSpec((None, None, n_q, n_keys), lambda b, i, ly: (0, variant(i, 1), 0, 0))
    bias1 = pl.BlockSpec((None, None, n_q, n_keys), lambda b, i, ly: (1, variant(i, nt - 1), 0, 0))
    tile = lambda b, i, ly: (b, i, 0)
    lay3 = lambda b, i, ly: (ly[0], 0, 0)
    flat = lambda b, i, ly: (b * nt + i, 0)
    kv_ctx = pl.BlockSpec((None, CTX_LEN, KV_W), lambda b, i, ly: (b, 0, 0))
    kv_prev = pl.BlockSpec((None, WINDOW, KV_W), lambda b, i, ly: (b, jnp.maximum(2 * i - 1, 0), 0))
    kv_this = pl.BlockSpec((None, TILE, KV_W), tile)
    kv_next = pl.BlockSpec((None, WINDOW, KV_W), lambda b, i, ly: (b, jnp.minimum(2 * i + 2, n_half - 1), 0))
    row_vec = lambda width: pl.BlockSpec((None, 1, width), lay3)
    return pl.pallas_call(
        _mixer_kernel,
        out_shape=(jax.ShapeDtypeStruct((bsz, n_tok, D_MODEL), F32),
                   jax.ShapeDtypeStruct((bsz * n_tok, D_MODEL // 2), jnp.uint32),
                   jax.ShapeDtypeStruct((bsz * n_tok, LANES), jnp.int32),
                   jax.ShapeDtypeStruct((bsz * n_tok, LANES), F32),
                   jax.ShapeDtypeStruct((SUBLANES, LANES), F32)),
        grid_spec=pltpu.PrefetchScalarGridSpec(
            num_scalar_prefetch=1,
            grid=(bsz, nt),
            in_specs=[
                pl.BlockSpec((None, TILE, D_MODEL), tile),
                pl.BlockSpec((None, None, N_MOD, D_MODEL),
                             lambda b, i, ly: (ly[0], jnp.where(i == 0, bsz, b), 0, 0)),
                pl.BlockSpec((None, D_MODEL, MAIN_W), lay3, pipeline_mode=once),
                row_vec(MAIN_W),
                pl.BlockSpec((TILE, KV_W), lambda b, i, ly: (i, 0)),
                pl.BlockSpec((TILE, KV_W), lambda b, i, ly: (i, 0)),
                pl.BlockSpec((None, 1, ATT_HEADS), lay3),
                bias0, bias1,
                kv_ctx, kv_ctx, kv_prev, kv_prev, kv_this, kv_this, kv_next, kv_next,
                pl.BlockSpec((None, TILE, LRU_W), tile),
                pl.BlockSpec((None, TILE, LRU_W), tile),
                row_vec(SGU_W), row_vec(SGU_W),
                pl.BlockSpec((None, SGU_GROUPS, CHUNK, CHUNK), lambda b, i, ly: (ly[0], 0, 0, 0)),
                pl.BlockSpec((None, CHUNK, SGU_W), lay3),
                pl.BlockSpec((None, SGU_W, D_MODEL), lay3, pipeline_mode=once),
                pl.BlockSpec((None, ATT_W, D_MODEL), lay3, pipeline_mode=once),
                pl.BlockSpec((None, LRU_W, D_MODEL), lay3, pipeline_mode=once),
                pl.BlockSpec((None, D_MODEL, D_MODEL), lay3, pipeline_mode=once),
                row_vec(D_MODEL), row_vec(D_MODEL), row_vec(D_MODEL),
                pl.BlockSpec((None, D_MODEL, LANES), lay3),
                row_vec(LANES),
            ],
            out_specs=(pl.BlockSpec((None, TILE, D_MODEL), tile),
                       pl.BlockSpec((TILE, D_MODEL // 2), flat),
                       pl.BlockSpec((TILE, LANES), flat),
                       pl.BlockSpec((TILE, LANES), flat),
                       pl.BlockSpec((SUBLANES, LANES), lambda b, i, ly: (0, 0))),
            scratch_shapes=[pltpu.VMEM((1, LANES), F32)],
        ),
        compiler_params=pltpu.CompilerParams(dimension_semantics=("arbitrary", "arbitrary"),
                                             vmem_limit_bytes=VMEM_LIMIT),
        name="mixer",
    )(lyr, xc, mods, w_main, b_main, cos_t, sin_t, sink, bias, bias, k, v, k, v, k, v, k, v, hf, hb,
      sg, sb, w_sp, b_sp, wa, wb, wc, wo, bo, l1g, l1b, w_rt, b_rt)


def _dispatch_kernel(slot_ref, h2_ref, buf_in_ref, buf_ref, tile_s, in_sems, out_sems, *, n_steps):
    del buf_in_ref
    s = pl.program_id(0)

    def load(step, p):
        return pltpu.make_async_copy(h2_ref.at[pl.ds(step * TILE, TILE)], tile_s.at[p], in_sems.at[p])

    def scatters(step, p):
        return [(k, pltpu.make_async_copy(tile_s.at[p, pl.ds(t, 1)],
                                          buf_ref.at[pl.ds(slot_ref[TOP_K * (step * TILE + t) + k], 1)],
                                          out_sems.at[p]))
                for t in range(TILE) for k in range(TOP_K)]

    @pl.when(s == 0)
    def _():
        load(s, 0).start()

    def step_body(p):
        @pl.when(s >= 1)
        def _():
            for _, cp in scatters(s - 1, 1 - p):
                cp.wait()

        @pl.when(s + 1 < n_steps)
        def _():
            load(s + 1, 1 - p).start()

        load(s, p).wait()
        for k, cp in scatters(s, p):
            cp.start(priority=k)

        @pl.when(s == n_steps - 1)
        def _():
            for _, cp in scatters(s, p):
                cp.wait()

    for p in range(2):
        pl.when(s % 2 == p)(functools.partial(step_body, p))


def _dispatch(slot, h2, n_slots):
    n_tok, width = h2.shape
    n_steps = n_tok // TILE
    return pl.pallas_call(
        functools.partial(_dispatch_kernel, n_steps=n_steps),
        out_shape=jax.ShapeDtypeStruct((n_slots, width), h2.dtype),
        grid_spec=pltpu.PrefetchScalarGridSpec(
            num_scalar_prefetch=1,
            grid=(n_steps,),
            in_specs=[pl.BlockSpec(memory_space=pl.ANY), pl.BlockSpec(memory_space=pl.ANY)],
            out_specs=pl.BlockSpec(memory_space=pl.ANY),
            scratch_shapes=[pltpu.VMEM((2, TILE, width), h2.dtype),
                            pltpu.SemaphoreType.DMA((2,)), pltpu.SemaphoreType.DMA((2,))],
        ),
        input_output_aliases={2: 0},
        compiler_params=pltpu.CompilerParams(dimension_semantics=("arbitrary",)),
        name="dispatch",
    )(slot, h2, jnp.zeros((n_slots, width), h2.dtype))


def _experts_kernel(ly_ref, be_ref, nu_ref, first_ref, slot_ref, next_ref, x_ref, w1_ref, w3_ref, w2_ref,
                    o_ref, w1s, w3s, w2s, w1f, w3f, w2f, sems):
    i = pl.program_id(0)
    layer = ly_ref[0]

    def loads(e, slot):
        return [pltpu.make_async_copy(w_ref.at[layer, e], stage.at[slot], sems.at[slot])
                for w_ref, stage in ((w1_ref, w1f), (w3_ref, w3f), (w2_ref, w2f))]

    @pl.when((i == 0) & (nu_ref[0] > 0))
    def _():
        for cp in loads(be_ref[0], 0):
            cp.start()

    @pl.when((i < nu_ref[0]) & (first_ref[i] == 1))
    def _():
        slot = slot_ref[i]
        for cp in loads(be_ref[i], slot):
            cp.wait()
        w1s[...] = w1f[slot].astype(BF16)
        w3s[...] = w3f[slot].astype(BF16)
        w2s[...] = w2f[slot].astype(BF16)

        @pl.when(next_ref[i] >= 0)
        def _():
            for cp in loads(next_ref[i], 1 - slot):
                cp.start()

    @pl.when(i < nu_ref[0])
    def _():
        x = _unpack_rows(x_ref[...]).astype(BF16)
        h1 = _dot(x, w1s[...])
        h3 = _dot(x, w3s[...])
        act = (h1 * _sigmoid(h1) * h3).astype(BF16)
        o_ref[...] = _pack_rows(_dot(act, w2s[...]))

    @pl.when(i >= nu_ref[0])
    def _():
        o_ref[...] = jnp.zeros_like(o_ref)


def _experts(lyr, block_expert, n_used, buf, w1, w3, w2):
    n_blocks = buf.shape[0] // EXPERT_BLOCK
    pos = jnp.arange(n_blocks, dtype=jnp.int32)
    first = ((pos == 0) | (block_expert != jnp.roll(block_expert, 1))) & (pos < n_used[0])
    run_slot = (jnp.cumsum(first.astype(jnp.int32)) - 1) % 2
    first_pos = jnp.where(first, pos, n_blocks)
    next_first = jnp.flip(lax.cummin(jnp.flip(jnp.roll(first_pos, -1).at[-1].set(n_blocks))))
    next_expert = jnp.where(next_first < n_blocks, block_expert[jnp.minimum(next_first, n_blocks - 1)], -1)

    def blk(i, ly, be, nu, fi, sl, ne):
        return (jnp.minimum(i, nu[0] - 1), 0)

    return pl.pallas_call(
        _experts_kernel,
        out_shape=jax.ShapeDtypeStruct(buf.shape, buf.dtype),
        grid_spec=pltpu.PrefetchScalarGridSpec(
            num_scalar_prefetch=6,
            grid=(n_blocks,),
            in_specs=[pl.BlockSpec((EXPERT_BLOCK, D_MODEL // 2), blk),
                      pl.BlockSpec(memory_space=pl.ANY),
                      pl.BlockSpec(memory_space=pl.ANY),
                      pl.BlockSpec(memory_space=pl.ANY)],
            out_specs=pl.BlockSpec((EXPERT_BLOCK, D_MODEL // 2), lambda i, ly, be, nu, fi, sl, ne: (i, 0)),
            scratch_shapes=[pltpu.VMEM((D_MODEL, D_EXPERT), BF16),
                            pltpu.VMEM((D_MODEL, D_EXPERT), BF16),
                            pltpu.VMEM((D_EXPERT, D_MODEL), BF16),
                            pltpu.VMEM((2, D_MODEL, D_EXPERT), F32),
                            pltpu.VMEM((2, D_MODEL, D_EXPERT), F32),
                            pltpu.VMEM((2, D_EXPERT, D_MODEL), F32),
                            pltpu.SemaphoreType.DMA((2,))],
        ),
        compiler_params=pltpu.CompilerParams(dimension_semantics=("arbitrary",),
                                             vmem_limit_bytes=40 * MIB),
        name="experts",
    )(lyr, block_expert, n_used, first.astype(jnp.int32), run_slot.astype(jnp.int32),
      next_expert.astype(jnp.int32), buf, w1, w3, w2)


def _combine_kernel(ly_ref, slot_ref, x1_ref, mod_ref, rf_ref, g_ref, b_ref, eo_ref, o_ref, gbuf, sems,
                    *, n_steps):
    del ly_ref
    s = pl.program_id(0)
    n_groups = TILE // GATHER_ROWS
    gate2 = mod_ref[5:6, :]
    ln_g, ln_b = g_ref[...], b_ref[...]

    def group_copies(tile, buf, j):
        copies = []
        for r in range(GATHER_ROWS):
            t = j * GATHER_ROWS + r
            for k in range(TOP_K):
                src = eo_ref.at[pl.ds(slot_ref[TOP_K * (tile * TILE + t) + k], 1)]
                copies.append((k, pltpu.make_async_copy(src, gbuf.at[buf, k, pl.ds(t, 1)], sems.at[buf, j])))
        return copies

    def start_group(tile, buf, j):
        for k, cp in group_copies(tile, buf, j):
            cp.start(priority=k)

    def wait_group(tile, buf, j):
        for _, cp in group_copies(tile, buf, j):
            cp.wait()

    @pl.when(s == 0)
    def _():
        for j in range(n_groups):
            start_group(s, 0, j)

    nxt = jnp.minimum(s + 1, n_steps - 1)

    def step_body(p):
        for j in range(n_groups):
            start_group(nxt, 1 - p, j)
            wait_group(s, p, j)
            rows = slice(j * GATHER_ROWS, (j + 1) * GATHER_ROWS)
            rf = rf_ref[rows, :]
            y = (rf[:, 0:1] * _unpack_rows(gbuf[p, 0, rows, :])
                 + rf[:, 1:2] * _unpack_rows(gbuf[p, 1, rows, :]))
            o_ref[rows, :] = _layer_norm(ALPHA * x1_ref[rows, :] + gate2 * y, ln_g, ln_b)

        @pl.when(s == n_steps - 1)
        def _():
            for j in range(n_groups):
                wait_group(nxt, 1 - p, j)

    for p in range(2):
        pl.when(s % 2 == p)(functools.partial(step_body, p))


def _combine(lyr, slot, x1, mods, rf, l2g, l2b, eo, latent_only):
    bsz, n_tok, _ = x1.shape
    nt = n_tok // TILE
    n_steps = bsz * nt
    tile = lambda s, ly, sl: (s // nt, s % nt, 0)
    lay3 = lambda s, ly, sl: (ly[0], 0, 0)
    if latent_only:
        out_rows = n_tok - CTX_LEN
        out_tile = lambda s, ly, sl: (s // nt, jnp.maximum(s % nt - 1, 0), 0)
    else:
        out_rows, out_tile = n_tok, tile
    return pl.pallas_call(
        functools.partial(_combine_kernel, n_steps=n_steps),
        out_shape=jax.ShapeDtypeStruct((bsz, out_rows, D_MODEL), F32),
        grid_spec=pltpu.PrefetchScalarGridSpec(
            num_scalar_prefetch=2,
            grid=(n_steps,),
            in_specs=[pl.BlockSpec((None, TILE, D_MODEL), tile),
                      pl.BlockSpec((None, None, N_MOD, D_MODEL),
                                   lambda s, ly, sl: (ly[0], jnp.where(s % nt == 0, bsz, s // nt), 0, 0)),
                      pl.BlockSpec((TILE, LANES), lambda s, ly, sl: (s, 0)),
                      pl.BlockSpec((None, 1, D_MODEL), lay3),
                      pl.BlockSpec((None, 1, D_MODEL), lay3),
                      pl.BlockSpec(memory_space=pl.ANY)],
            out_specs=pl.BlockSpec((None, TILE, D_MODEL), out_tile),
            scratch_shapes=[pltpu.VMEM((2, TOP_K, TILE, D_MODEL // 2), jnp.uint32),
                            pltpu.SemaphoreType.DMA((2, TILE // GATHER_ROWS))],
        ),
        compiler_params=pltpu.CompilerParams(dimension_semantics=("arbitrary",),
                                             vmem_limit_bytes=40 * MIB),
        name="combine",
    )(lyr, slot, x1, mods, rf, l2g, l2b, eo)


def _slot_tables(route_i, counts, n_blocks):
    counts = counts.astype(jnp.int32)
    padded = (counts + EXPERT_BLOCK - 1) // EXPERT_BLOCK * EXPERT_BLOCK
    pad_ends = jnp.cumsum(padded)
    pad_starts = pad_ends - padded
    expert = route_i[:, :TOP_K]
    rank = route_i[:, TOP_K:2 * TOP_K]
    is_e = expert[..., None] == jnp.arange(N_EXPERTS, dtype=jnp.int32)
    slot = (jnp.sum(jnp.where(is_e, pad_starts, 0), axis=-1) + rank).reshape(-1)
    block_row = jnp.arange(n_blocks, dtype=jnp.int32) * EXPERT_BLOCK
    block_expert = jnp.minimum(jnp.sum((pad_ends[None, :] <= block_row[:, None]).astype(jnp.int32), axis=1),
                               N_EXPERTS - 1)
    n_used = (pad_ends[-1:] // EXPERT_BLOCK).astype(jnp.int32)
    return slot, block_expert, n_used


def _rope_tables(n_lat):
    rows = n_lat // GRID_W
    row = jnp.repeat(jnp.arange(rows, dtype=F32), GRID_W)
    col = jnp.tile(jnp.arange(GRID_W, dtype=F32), rows)
    axis_dim = HEAD_DIM // 2
    inv_freq = ROPE_THETA ** (-jnp.arange(0, axis_dim, 2, dtype=F32) / axis_dim)
    ang_r = row[:, None] * inv_freq
    ang_c = col[:, None] * inv_freq
    cos = jnp.concatenate([jnp.cos(ang_r)] * 2 + [jnp.cos(ang_c)] * 2, axis=1)
    sin = jnp.concatenate([-jnp.sin(ang_r), jnp.sin(ang_r), -jnp.sin(ang_c), jnp.sin(ang_c)], axis=1)
    cos = jnp.concatenate([jnp.ones((CTX_LEN, HEAD_DIM), F32), cos], axis=0)
    sin = jnp.concatenate([jnp.zeros((CTX_LEN, HEAD_DIM), F32), sin], axis=0)
    return jnp.tile(cos, (1, KV_HEADS)), jnp.tile(sin, (1, KV_HEADS))


def kernel(x, c, ctx, c_ctx, w_mod, b_mod, w_in, b_in, sgu_ln_g, sgu_ln_b, w_spatial, b_spatial, attn_sink, conv_w, conv_b, w_rgate, b_rgate, w_igate, b_igate, lru_lambda, w_proj_a, w_proj_b, w_proj_c, w_out, b_out, ln1_g, ln1_b, ln2_g, ln2_b, w_group, b_group, w_router, b_router, w1, w3, w2):
    bsz, n_lat, _ = x.shape
    n_layers = w_mod.shape[0]
    assert ctx.shape[1] == CTX_LEN == TILE and n_lat % TILE == 0 and bsz < SUBLANES
    n_tok = CTX_LEN + n_lat
    n_blocks = -(-bsz * n_tok * TOP_K // EXPERT_BLOCK) + N_EXPERTS
    n_slots = n_blocks * EXPERT_BLOCK

    cond = jnp.zeros((SUBLANES, D_MODEL), F32).at[:bsz].set(c).at[bsz].set(c_ctx)
    mods = _modulation(cond, w_mod, b_mod).reshape(n_layers, SUBLANES, N_MOD, D_MODEL)
    w_main = w_in[:, :, :MAIN_W].astype(BF16)
    w_ctx = w_in[:, :, MAIN_W:].astype(BF16)
    b_main = b_in[:, None, :MAIN_W]
    b_ctx = b_in[:, None, MAIN_W:]
    w_ri = jnp.concatenate([w_rgate, w_igate], axis=-1).astype(BF16)
    w_sp = w_spatial.astype(BF16)
    b_sp = jnp.repeat(jnp.swapaxes(b_spatial, 1, 2), CHUNK, axis=2)
    wa, wb, wc, wo = (w.astype(BF16) for w in (w_proj_a, w_proj_b, w_proj_c, w_out))
    w_rt = jnp.zeros((n_layers, D_MODEL, LANES), F32)
    w_rt = w_rt.at[:, :, :N_GROUPS].set(w_group).at[:, :, N_GROUPS:N_GROUPS + N_EXPERTS].set(w_router)
    b_rt = jnp.zeros((n_layers, 1, LANES), F32)
    b_rt = b_rt.at[:, 0, :N_GROUPS].set(b_group).at[:, 0, N_GROUPS:N_GROUPS + N_EXPERTS].set(b_router)
    cos_t, sin_t = _rope_tables(n_lat)
    row3 = lambda a: a[:, None, :]
    conv_b4, b_r4, b_i4 = (a[:, :, None, :] for a in (conv_b, b_rgate, b_igate))
    sink, sg, sb, bo = row3(attn_sink), row3(sgu_ln_g), row3(sgu_ln_b), row3(b_out)
    l1g, l1b, l2g, l2b = row3(ln1_g), row3(ln1_b), row3(ln2_g), row3(ln2_b)

    def layer(l, xc):
        lyr = jnp.full((1,), l, jnp.int32)
        k, v, hf, hb = _kv_lru(lyr, xc, mods, w_ctx, b_ctx, cos_t, sin_t, conv_w, conv_b4,
                               w_ri, b_r4, b_i4, lru_lambda)
        x1, h2, route_i, route_f, counts = _mixer(
            lyr, xc, mods, w_main, b_main, cos_t, sin_t, sink, k, v, hf, hb,
            sg, sb, w_sp, b_sp, wa, wb, wc, wo, bo, l1g, l1b, w_rt, b_rt)
        slot, block_expert, n_used = _slot_tables(route_i, counts[0, :N_EXPERTS], n_blocks)
        buf = _dispatch(slot, h2, n_slots)
        eo = _experts(lyr, block_expert, n_used, buf, w1, w3, w2)
        return _combine(lyr, slot, x1, mods, route_f, l2g, l2b, eo, latent_only=l == n_layers - 1)

    xc = jnp.concatenate([ctx, x], axis=1)
    for l in range(n_layers):
        xc = layer(l, xc)
    return xc
```
